```python
import math
import jax, jax.numpy as jnp
from jax import lax
import numpy as np

D_MODEL = 1024
BATCH = 4
SEQ = 4096
DEPTH = 4

HEAD_DIM = 64
BLOCK = 128
A_HEADS = 8
A_KV = 2
WINDOW = 128
B_HEADS = 8
B_KV = 2
GRID_W = 64
C_WIDTH = 512
C_GROUPS = 4
C_GROUP_DIM = C_WIDTH // C_GROUPS
CHUNK = 128
N_BRANCH = 3
BRANCH_WIDTH = 512
ROPE_THETA = 10000.0
MEM_LEN = 256
X_HEADS = 4
X_HEAD_DIM = 128
X_WIDTH = X_HEADS * X_HEAD_DIM
D_FF = 2816
CONV_W = 3
ALPHA = (2 * DEPTH) ** 0.25
BETA = (8 * DEPTH) ** -0.25
LN_EPS = 1e-5
RMS_EPS = 1e-6

A_Q_W = A_HEADS * HEAD_DIM
A_KV_W = A_KV * HEAD_DIM
B_Q_W = B_HEADS * HEAD_DIM
B_KV_W = B_KV * HEAD_DIM
IN_SIZES = (A_Q_W, A_KV_W, A_KV_W, B_Q_W, B_KV_W, B_KV_W, 2 * C_WIDTH, N_BRANCH * D_MODEL)
D_IN = sum(IN_SIZES)

kernel_name = "hybrid_gated_window_axial_gmlp_encoder"


def _split_points(sizes):
    pts, acc = [], 0
    for s in sizes[:-1]:
        acc += s
        pts.append(acc)
    return pts


def layer_norm(x, g, b):
    xf = x.astype(jnp.float32)
    mu = jnp.mean(xf, axis=-1, keepdims=True)
    var = jnp.mean(jnp.square(xf - mu), axis=-1, keepdims=True)
    y = (xf - mu) * lax.rsqrt(var + LN_EPS) * g.astype(jnp.float32) + b.astype(jnp.float32)
    return y.astype(x.dtype)


def rms_norm(x, g):
    xf = x.astype(jnp.float32)
    y = xf * lax.rsqrt(jnp.mean(jnp.square(xf), axis=-1, keepdims=True) + RMS_EPS)
    return (y * g.astype(jnp.float32)).astype(x.dtype)


def rope(x, pos, theta):
    d = x.shape[-1]
    half = d // 2
    inv = theta ** (-jnp.arange(half, dtype=jnp.float32) * (2.0 / d))
    ang = pos.astype(jnp.float32)[:, None] * inv[None, :]
    cos = jnp.cos(ang)[:, None, :]
    sin = jnp.sin(ang)[:, None, :]
    x1 = x[..., :half].astype(jnp.float32)
    x2 = x[..., half:].astype(jnp.float32)
    out = jnp.concatenate([x1 * cos - x2 * sin, x2 * cos + x1 * sin], axis=-1)
    return out.astype(x.dtype)


def window_attention(q, k, v, sink):
    B, S, H, D = q.shape
    KV = k.shape[2]
    G = H // KV
    nb = S // BLOCK
    qb = q.reshape(B, nb, BLOCK, KV, G, D)
    pad = ((0, 0), (BLOCK, BLOCK), (0, 0), (0, 0))
    kp = jnp.pad(k, pad).reshape(B, nb + 2, BLOCK, KV, D)
    vp = jnp.pad(v, pad).reshape(B, nb + 2, BLOCK, KV, D)
    kw = jnp.concatenate([kp[:, :-2], kp[:, 1:-1], kp[:, 2:]], axis=2)
    vw = jnp.concatenate([vp[:, :-2], vp[:, 1:-1], vp[:, 2:]], axis=2)
    s = jnp.einsum('bnqkgd,bnskd->bnkgqs', qb, kw).astype(jnp.float32) / math.sqrt(D)
    blk = jnp.arange(nb, dtype=jnp.int32)[:, None]
    qpos = blk * BLOCK + jnp.arange(BLOCK, dtype=jnp.int32)[None, :]
    kpos = (blk - 1) * BLOCK + jnp.arange(3 * BLOCK, dtype=jnp.int32)[None, :]
    dist = jnp.abs(qpos[:, :, None] - kpos[:, None, :])
    valid = (dist <= WINDOW) & (kpos[:, None, :] >= 0) & (kpos[:, None, :] < S)
    s = jnp.where(valid[None, :, None, None], s, -jnp.inf)
    sink_l = jnp.broadcast_to(sink.astype(jnp.float32).reshape(KV, G)[None, None, :, :, None, None],
                              s.shape[:-1] + (1,))
    p = jax.nn.softmax(jnp.concatenate([s, sink_l], axis=-1), axis=-1)[..., :-1]
    o = jnp.einsum('bnkgqs,bnskd->bnqkgd', p.astype(v.dtype), vw)
    return o.reshape(B, S, H * D)


def dense_block_attention(q, k, v):
    B, S, H, D = q.shape
    KV = k.shape[2]
    G = H // KV
    nb = S // BLOCK
    scale = 1.0 / math.sqrt(D)
    qb = q.reshape(B, nb, BLOCK, KV, G, D).transpose(1, 0, 2, 3, 4, 5)

    def one_block(qblk):
        s = jnp.einsum('bqkgd,bskd->bkgqs', qblk, k).astype(jnp.float32) * scale
        p = jax.nn.softmax(s, axis=-1)
        return jnp.einsum('bkgqs,bskd->bqkgd', p.astype(v.dtype), v)

    o = lax.map(one_block, qb)
    return o.transpose(1, 0, 2, 3, 4, 5).reshape(B, S, H * D)


def spatial_gating(u, v, ln_g, ln_b, w_s, b_s):
    B, S, _ = v.shape
    nc = S // CHUNK
    vc = layer_norm(v, ln_g, ln_b).reshape(B, nc, CHUNK, C_GROUPS, C_GROUP_DIM)
    mixed = jnp.einsum('gij,bnjgc->bnigc', w_s, vc) + b_s.T[None, None, :, :, None]
    return u * mixed.reshape(B, S, C_WIDTH)


def hybrid_mixer(x, pos, row, col, w_in, b_gate, a_sink, b_q_gain, b_k_gain,
                 c_ln_g, c_ln_b, c_ws, c_bs, w_branch, w_mix_out):
    B, S, _ = x.shape
    proj = x @ w_in
    aq, ak, av, bq, bk, bv, cz, gate = jnp.split(proj, _split_points(IN_SIZES), axis=-1)
    aq = rope(aq.reshape(B, S, A_HEADS, HEAD_DIM), pos, ROPE_THETA)
    ak = rope(ak.reshape(B, S, A_KV, HEAD_DIM), pos, ROPE_THETA)
    av = av.reshape(B, S, A_KV, HEAD_DIM)
    out_a = window_attention(aq, ak, av, a_sink)
    half = HEAD_DIM // 2
    bq = rms_norm(bq.reshape(B, S, B_HEADS, HEAD_DIM), b_q_gain)
    bk = rms_norm(bk.reshape(B, S, B_KV, HEAD_DIM), b_k_gain)
    bq = jnp.concatenate([rope(bq[..., :half], row, ROPE_THETA), rope(bq[..., half:], col, ROPE_THETA)], axis=-1)
    bk = jnp.concatenate([rope(bk[..., :half], row, ROPE_THETA), rope(bk[..., half:], col, ROPE_THETA)], axis=-1)
    bv = bv.reshape(B, S, B_KV, HEAD_DIM)
    out_b = dense_block_attention(bq, bk, bv)
    u, v = jnp.split(jax.nn.gelu(cz, approximate=False), 2, axis=-1)
    out_c = spatial_gating(u, v, c_ln_g, c_ln_b, c_ws, c_bs)
    branches = jnp.stack([out_a, out_b, out_c], axis=2)
    gates = jax.nn.sigmoid(gate + b_gate).reshape(B, S, N_BRANCH, D_MODEL)
    merged = (jnp.einsum('bsnc,ncd->bsnd', branches, w_branch) * gates).sum(axis=2)
    return merged @ w_mix_out


def memory_cross_attention(x, mem, wq, wkv, wo):
    B, S, _ = x.shape
    M = mem.shape[1]
    q = (x @ wq).reshape(B, S, X_HEADS, X_HEAD_DIM)
    k, v = jnp.split(mem @ wkv, 2, axis=-1)
    k = k.reshape(B, M, X_HEADS, X_HEAD_DIM)
    v = v.reshape(B, M, X_HEADS, X_HEAD_DIM)
    s = jnp.einsum('bqhd,bmhd->bhqm', q, k).astype(jnp.float32) / math.sqrt(X_HEAD_DIM)
    p = jax.nn.softmax(s, axis=-1)
    o = jnp.einsum('bhqm,bmhd->bqhd', p.astype(v.dtype), v)
    return o.reshape(B, S, X_WIDTH) @ wo


def conv_ffn(x, w_up, conv_k, conv_b, w_down):
    S = x.shape[1]
    h = x @ w_up
    r = CONV_W // 2
    hp = jnp.pad(h, ((0, 0), (r, r), (0, 0)))
    h = sum(hp[:, j:j + S] * conv_k[j] for j in range(CONV_W)) + conv_b
    a, b = jnp.split(h, 2, axis=-1)
    return (jax.nn.gelu(a, approximate=False) * b) @ w_down


def setup_inputs(seed: int = 0) -> dict:
    key = jax.random.key(seed)
    ks = jax.random.split(key, 26)

    def nrm(k, shape, scale):
        return jax.random.normal(k, shape, dtype=jnp.float32) * scale

    L = DEPTH
    return {
        "x": nrm(ks[0], (BATCH, SEQ, D_MODEL), 1.0),
        "mem": nrm(ks[1], (BATCH, MEM_LEN, D_MODEL), 1.0),
        "w_in": nrm(ks[2], (L, D_MODEL, D_IN), D_MODEL ** -0.5),
        "b_gate": nrm(ks[3], (L, N_BRANCH * D_MODEL), 0.1),
        "a_sink": nrm(ks[4], (L, A_HEADS), 1.0),
        "b_q_gain": 1.0 + nrm(ks[5], (L, HEAD_DIM), 0.02),
        "b_k_gain": 1.0 + nrm(ks[6], (L, HEAD_DIM), 0.02),
        "c_ln_g": 1.0 + nrm(ks[7], (L, C_WIDTH), 0.02),
        "c_ln_b": nrm(ks[8], (L, C_WIDTH), 0.02),
        "c_ws": nrm(ks[9], (L, C_GROUPS, CHUNK, CHUNK), CHUNK ** -0.5),
        "c_bs": 1.0 + nrm(ks[10], (L, C_GROUPS, CHUNK), 0.02),
        "w_branch": nrm(ks[11], (L, N_BRANCH, BRANCH_WIDTH, D_MODEL), BRANCH_WIDTH ** -0.5),
        "w_mix_out": nrm(ks[12], (L, D_MODEL, D_MODEL), BETA * D_MODEL ** -0.5),
        "ln1_g": 1.0 + nrm(ks[13], (L, D_MODEL), 0.02),
        "ln1_b": nrm(ks[14], (L, D_MODEL), 0.02),
        "x_wq": nrm(ks[15], (L, D_MODEL, X_WIDTH), D_MODEL ** -0.5),
        "x_wkv": nrm(ks[16], (L, D_MODEL, 2 * X_WIDTH), D_MODEL ** -0.5),
        "x_wo": nrm(ks[17], (L, X_WIDTH, D_MODEL), BETA * X_WIDTH ** -0.5),
        "ln2_g": 1.0 + nrm(ks[18], (L, D_MODEL), 0.02),
        "ln2_b": nrm(ks[19], (L, D_MODEL), 0.02),
        "f_w_up": nrm(ks[20], (L, D_MODEL, 2 * D_FF), D_MODEL ** -0.5),
        "f_conv_k": nrm(ks[21], (L, CONV_W, 2 * D_FF), CONV_W ** -0.5),
        "f_conv_b": nrm(ks[22], (L, 2 * D_FF), 0.02),
        "f_w_down": nrm(ks[23], (L, D_FF, D_MODEL), BETA * D_FF ** -0.5),
        "ln3_g": 1.0 + nrm(ks[24], (L, D_MODEL), 0.02),
        "ln3_b": nrm(ks[25], (L, D_MODEL), 0.02),
    }


def reference(x, mem, w_in, b_gate, a_sink, b_q_gain, b_k_gain, c_ln_g, c_ln_b, c_ws, c_bs,
              w_branch, w_mix_out, ln1_g, ln1_b, x_wq, x_wkv, x_wo, ln2_g, ln2_b,
              f_w_up, f_conv_k, f_conv_b, f_w_down, ln3_g, ln3_b):
    seq = x.shape[1]
    rows = seq // GRID_W
    pos = jnp.arange(seq, dtype=jnp.int32)
    row = jnp.repeat(jnp.arange(rows, dtype=jnp.int32), GRID_W)
    col = jnp.tile(jnp.arange(GRID_W, dtype=jnp.int32), rows)
    for l in range(DEPTH):
        h = hybrid_mixer(x, pos, row, col, w_in[l], b_gate[l], a_sink[l], b_q_gain[l], b_k_gain[l],
                         c_ln_g[l], c_ln_b[l], c_ws[l], c_bs[l], w_branch[l], w_mix_out[l])
        x = layer_norm(ALPHA * x + h, ln1_g[l], ln1_b[l])
        h = memory_cross_attention(x, mem, x_wq[l], x_wkv[l], x_wo[l])
        x = layer_norm(ALPHA * x + h, ln2_g[l], ln2_b[l])
        h = conv_ffn(x, f_w_up[l], f_conv_k[l], f_conv_b[l], f_w_down[l])
        x = layer_norm(ALPHA * x + h, ln3_g[l], ln3_b[l])
    return x
```

```python
import functools
import math

import jax
import jax.numpy as jnp
from jax import lax
from jax.experimental import pallas as pl
from jax.experimental.pallas import tpu as pltpu

F32 = jnp.float32
BF16 = jnp.bfloat16

D_MODEL = 1024
HEAD_DIM = 64
BLOCK = 128
A_HEADS = 8
A_KV = 2
WINDOW = 128
B_HEADS = 8
B_KV = 2
GRID_W = 64
C_WIDTH = 512
C_GROUPS = 4
CHUNK = 128
N_BRANCH = 3
ROPE_THETA = 10000.0
X_HEADS = 4
X_HEAD_DIM = 128
X_WIDTH = X_HEADS * X_HEAD_DIM
D_FF = 2816
CONV_W = 3
LN_EPS = 1e-5
RMS_EPS = 1e-6

A_Q_W = A_HEADS * HEAD_DIM
A_KV_W = A_KV * HEAD_DIM
B_Q_W = B_HEADS * HEAD_DIM
B_KV_W = B_KV * HEAD_DIM
QKV_W = A_Q_W + 2 * A_KV_W
BRANCH_IN_W = 2 * QKV_W + 2 * C_WIDTH
GATE_W = N_BRANCH * D_MODEL

LANES = 128
ROW_TILE = 512
ATTN_B_TQ = 256
ATTN_A_QBLOCKS = 2
FF_CHUNK = 256
HALO = 16
VMEM_LIMIT = 56 * 1024 * 1024
NEG_BIG = -1e30


def _dot(a, b):
    return jnp.dot(a, b, preferred_element_type=F32)


def _layer_norm(y, g, b):
    mu = jnp.mean(y, axis=-1, keepdims=True)
    d = y - mu
    var = jnp.mean(d * d, axis=-1, keepdims=True)
    return d * lax.rsqrt(var + LN_EPS) * g + b


def _gelu(x):
    return 0.5 * x * (1.0 + lax.erf(x * (1.0 / math.sqrt(2.0))))


def _resident(shape, index_map):
    return pl.BlockSpec(shape, index_map, pipeline_mode=pl.Buffered(1))


def _params(sem):
    return pltpu.CompilerParams(dimension_semantics=sem, vmem_limit_bytes=VMEM_LIMIT)


def _inproj_kernel(x_ref, w_ref, cosa_ref, sina_ref, cosb_ref, sinb_ref, qg_ref, kg_ref, bd_ref,
                   lng_ref, lnb_ref, ws_ref, bsb_ref,
                   aq_ref, akt_ref, av_ref, bq_ref, bkt_ref, bv_ref, oc_ref):
    tm = x_ref.shape[0]
    xb = x_ref[...].astype(BF16)
    lane = lax.broadcasted_iota(jnp.int32, (tm, LANES), 1)

    pa = _dot(xb, w_ref[:, 0:QKV_W])
    cosa = cosa_ref[...]
    sina = sina_ref[...]
    first_a = (lane & (HEAD_DIM - 1)) < (HEAD_DIM // 2)

    def rope_a(seg):
        sw = jnp.where(first_a, pltpu.roll(seg, LANES - HEAD_DIM // 2, 1), pltpu.roll(seg, HEAD_DIM // 2, 1))
        return seg * cosa + sw * sina

    scale = 1.0 / math.sqrt(HEAD_DIM)
    for j in range(A_Q_W // LANES):
        sl = slice(j * LANES, (j + 1) * LANES)
        aq_ref[:, sl] = (rope_a(pa[:, sl]) * scale).astype(BF16)
    akt_ref[...] = rope_a(pa[:, A_Q_W:A_Q_W + A_KV_W]).T.astype(BF16)
    av_ref[...] = pa[:, A_Q_W + A_KV_W:QKV_W].astype(BF16)

    pb = _dot(xb, w_ref[:, QKV_W:2 * QKV_W])
    cosb = cosb_ref[...]
    sinb = sinb_ref[...]
    quarter = HEAD_DIM // 4
    first_b = (lane & (2 * quarter - 1)) < quarter

    def rope_b(seg):
        sw = jnp.where(first_b, pltpu.roll(seg, LANES - quarter, 1), pltpu.roll(seg, quarter, 1))
        return seg * cosb + sw * sinb

    def head_sumsq(v, bd):
        sq = v * v
        hi = sq.astype(BF16)
        lo = (sq - hi.astype(F32)).astype(BF16)
        return _dot(hi, bd) + _dot(lo, bd)

    bq = pb[:, 0:B_Q_W]
    bqn = bq * lax.rsqrt(head_sumsq(bq, bd_ref[...]) * (1.0 / HEAD_DIM) + RMS_EPS) * qg_ref[...]
    for j in range(B_Q_W // LANES):
        sl = slice(j * LANES, (j + 1) * LANES)
        bq_ref[:, sl] = (rope_b(bqn[:, sl]) * scale).astype(BF16)
    bk = pb[:, B_Q_W:B_Q_W + B_KV_W]
    bkn = bk * lax.rsqrt(head_sumsq(bk, bd_ref[0:B_KV_W, 0:B_KV_W]) * (1.0 / HEAD_DIM) + RMS_EPS) * kg_ref[...]
    bkt_ref[...] = rope_b(bkn).T.astype(BF16)
    bv_ref[...] = pb[:, B_Q_W + B_KV_W:QKV_W].astype(BF16)

    pc = _gelu(_dot(xb, w_ref[:, 2 * QKV_W:BRANCH_IN_W]))
    u = pc[:, 0:C_WIDTH]
    vln = _layer_norm(pc[:, C_WIDTH:2 * C_WIDTH], lng_ref[...], lnb_ref[...]).astype(BF16)
    nch = tm // CHUNK
    gw = C_WIDTH // C_GROUPS
    for g in range(C_GROUPS):
        cs = slice(g * gw, (g + 1) * gw)
        rhs = jnp.concatenate([vln[r * CHUNK:(r + 1) * CHUNK, cs] for r in range(nch)], axis=1)
        mixed = _dot(ws_ref[g], rhs)
        for r in range(nch):
            rs = slice(r * CHUNK, (r + 1) * CHUNK)
            oc_ref[rs, cs] = (u[rs, cs] * (mixed[:, r * gw:(r + 1) * gw] + bsb_ref[g])).astype(BF16)


def _inproj(x, w, tabs, qg, kg, bd, lng, lnb, ws, bsb, layer, batch, seq, tm):
    n = x.shape[0]
    spb = seq // tm
    row = lambda i: (i, 0)
    tab = lambda i: (i % spb, 0)
    lay3 = lambda i: (layer, 0, 0)
    lay4 = lambda i: (layer, 0, 0, 0)
    tr = lambda i: (i // spb, 0, i % spb)
    out_shape = (
        jax.ShapeDtypeStruct((n, A_Q_W), BF16),
        jax.ShapeDtypeStruct((batch, A_KV_W, seq), BF16),
        jax.ShapeDtypeStruct((n, A_KV_W), BF16),
        jax.ShapeDtypeStruct((n, B_Q_W), BF16),
        jax.ShapeDtypeStruct((batch, B_KV_W, seq), BF16),
        jax.ShapeDtypeStruct((n, B_KV_W), BF16),
        jax.ShapeDtypeStruct((n, C_WIDTH), BF16),
    )
    in_specs = [
        pl.BlockSpec((tm, D_MODEL), row),
        _resident((None, D_MODEL, BRANCH_IN_W), lay3),
        pl.BlockSpec((tm, LANES), tab), pl.BlockSpec((tm, LANES), tab),
        pl.BlockSpec((tm, LANES), tab), pl.BlockSpec((tm, LANES), tab),
        _resident((None, 1, B_Q_W), lay3), _resident((None, 1, B_KV_W), lay3),
        _resident((B_Q_W, B_Q_W), lambda i: (0, 0)),
        _resident((None, 1, C_WIDTH), lay3), _resident((None, 1, C_WIDTH), lay3),
        _resident((None, C_GROUPS, CHUNK, CHUNK), lay4),
        _resident((None, C_GROUPS, CHUNK, C_WIDTH // C_GROUPS), lay4),
    ]
    out_specs = (
        pl.BlockSpec((tm, A_Q_W), row),
        pl.BlockSpec((None, A_KV_W, tm), tr),
        pl.BlockSpec((tm, A_KV_W), row),
        pl.BlockSpec((tm, B_Q_W), row),
        pl.BlockSpec((None, B_KV_W, tm), tr),
        pl.BlockSpec((tm, B_KV_W), row),
        pl.BlockSpec((tm, C_WIDTH), row),
    )
    return pl.pallas_call(
        _inproj_kernel, grid=(n // tm,), in_specs=in_specs, out_specs=out_specs, out_shape=out_shape,
        compiler_params=_params(("parallel",)), name="inproj",
    )(x, w, *tabs, qg, kg, bd, lng, lnb, ws, bsb)


def _attn_a_kernel(sink_ref, q_ref, ktl_ref, ktc_ref, ktr_ref, vl_ref, vc_ref, vr_ref, o_ref):
    tq = q_ref.shape[0]
    span = tq + 2 * BLOCK
    step = pl.program_id(1)
    last = pl.num_programs(1) - 1
    qi = lax.broadcasted_iota(jnp.int32, (tq, span), 0)
    kj = lax.broadcasted_iota(jnp.int32, (tq, span), 1)
    lo = jnp.where(step == 0, BLOCK, 0)
    hi = jnp.where(step == last, tq + BLOCK, span)
    valid = (kj >= qi) & (kj <= qi + 2 * WINDOW) & (kj >= lo) & (kj < hi)
    vcat = jnp.concatenate([vl_ref[...], vc_ref[...], vr_ref[...]], axis=0)
    group = A_HEADS // A_KV
    outs = []
    for g in range(A_KV):
        rows = slice(g * HEAD_DIM, (g + 1) * HEAD_DIM)
        kcat = jnp.concatenate([ktl_ref[rows, :], ktc_ref[rows, :], ktr_ref[rows, :]], axis=1)
        for h in range(group):
            head = g * group + h
            q = q_ref[:, head * HEAD_DIM:(head + 1) * HEAD_DIM]
            s = jnp.where(valid, _dot(q, kcat), NEG_BIG)
            sink = sink_ref[head]
            m = jnp.maximum(jnp.max(s, axis=-1, keepdims=True), sink)
            p = jnp.exp(s - m)
            denom = jnp.sum(p, axis=-1, keepdims=True) + jnp.exp(sink - m)
            o = _dot(p.astype(BF16), vcat)[:, rows]
            outs.append(o / denom)
    o_ref[...] = jnp.concatenate(outs, axis=1).astype(BF16)


def _attn_a(sink, q, kt, v, layer, batch, seq):
    n = q.shape[0]
    qb = ATTN_A_QBLOCKS if seq % (ATTN_A_QBLOCKS * BLOCK) == 0 else 1
    tq = qb * BLOCK
    steps = seq // tq
    nblk = seq // BLOCK
    left = lambda s: jnp.maximum(s * qb - 1, 0)
    right = lambda s: jnp.minimum((s + 1) * qb, nblk - 1)
    in_specs = [
        pl.BlockSpec(memory_space=pltpu.SMEM),
        pl.BlockSpec((tq, A_Q_W), lambda b, s: (b * steps + s, 0)),
        pl.BlockSpec((None, A_KV_W, BLOCK), lambda b, s: (b, 0, left(s))),
        pl.BlockSpec((None, A_KV_W, tq), lambda b, s: (b, 0, s)),
        pl.BlockSpec((None, A_KV_W, BLOCK), lambda b, s: (b, 0, right(s))),
        pl.BlockSpec((BLOCK, A_KV_W), lambda b, s: (b * nblk + left(s), 0)),
        pl.BlockSpec((tq, A_KV_W), lambda b, s: (b * steps + s, 0)),
        pl.BlockSpec((BLOCK, A_KV_W), lambda b, s: (b * nblk + right(s), 0)),
    ]
    return pl.pallas_call(
        _attn_a_kernel, grid=(batch, steps), in_specs=in_specs,
        out_specs=pl.BlockSpec((tq, A_Q_W), lambda b, s: (b * steps + s, 0)),
        out_shape=jax.ShapeDtypeStruct((n, A_Q_W), BF16),
        compiler_params=_params(("parallel", "parallel")), name="attn_window",
    )(sink, q, kt, kt, kt, v, v, v)


def _attn_b_kernel(q_ref, kt_ref, v_ref, o_ref):
    g = pl.program_id(1)
    kt = kt_ref[...]
    v = v_ref[...]
    outs = []
    for h in range(B_HEADS // B_KV):
        q = q_ref[:, h * HEAD_DIM:(h + 1) * HEAD_DIM]
        s = _dot(q, kt)
        m = jnp.max(s, axis=-1, keepdims=True)
        p = jnp.exp(s - m)
        denom = jnp.sum(p, axis=-1, keepdims=True)
        o2 = _dot(p.astype(BF16), v)
        o = jnp.where(g == 0, o2[:, 0:HEAD_DIM], o2[:, HEAD_DIM:2 * HEAD_DIM])
        outs.append(o / denom)
    o_ref[...] = jnp.concatenate(outs, axis=1).astype(BF16)


def _attn_b(q, kt, v, batch, seq):
    n = q.shape[0]
    tq = min(ATTN_B_TQ, seq)
    steps = seq // tq
    gw = B_Q_W // B_KV
    in_specs = [
        pl.BlockSpec((tq, gw), lambda b, g, s: (b * steps + s, g)),
        pl.BlockSpec((None, HEAD_DIM, seq), lambda b, g, s: (b, g, 0)),
        pl.BlockSpec((seq, B_KV_W), lambda b, g, s: (b, 0)),
    ]
    return pl.pallas_call(
        _attn_b_kernel, grid=(batch, B_KV, steps), in_specs=in_specs,
        out_specs=pl.BlockSpec((tq, gw), lambda b, g, s: (b * steps + s, g)),
        out_shape=jax.ShapeDtypeStruct((n, B_Q_W), BF16),
        compiler_params=_params(("parallel", "parallel", "parallel")), name="attn_dense",
    )(q, kt, v)


def _merge_kernel(alpha, x_ref, a_ref, b_ref, c_ref, wg_ref, bg_ref, wb_ref, wo_ref, g_ref, beta_ref, o_ref):
    x = x_ref[...]
    xb = x.astype(BF16)
    merged = None
    for i, br in enumerate((a_ref, b_ref, c_ref)):
        cols = slice(i * D_MODEL, (i + 1) * D_MODEL)
        gate = jax.nn.sigmoid(_dot(xb, wg_ref[:, cols]) + bg_ref[:, cols])
        t = _dot(br[...], wb_ref[i]) * gate
        merged = t if merged is None else merged + t
    h = _dot(merged.astype(BF16), wo_ref[...])
    o_ref[...] = _layer_norm(alpha * x + h, g_ref[...], beta_ref[...])


def _merge(alpha, x, oa, ob, oc, wg, bg, wb, wo, lg, lb, layer, tm):
    n = x.shape[0]
    row = lambda i: (i, 0)
    lay3 = lambda i: (layer, 0, 0)
    lay4 = lambda i: (layer, 0, 0, 0)
    in_specs = [
        pl.BlockSpec((tm, D_MODEL), row),
        pl.BlockSpec((tm, A_Q_W), row), pl.BlockSpec((tm, B_Q_W), row), pl.BlockSpec((tm, C_WIDTH), row),
        _resident((None, D_MODEL, GATE_W), lay3),
        _resident((None, 1, GATE_W), lay3),
        _resident((None, N_BRANCH, C_WIDTH, D_MODEL), lay4),
        _resident((None, D_MODEL, D_MODEL), lay3),
        _resident((None, 1, D_MODEL), lay3), _resident((None, 1, D_MODEL), lay3),
    ]
    return pl.pallas_call(
        functools.partial(_merge_kernel, alpha), grid=(n // tm,), in_specs=in_specs,
        out_specs=pl.BlockSpec((tm, D_MODEL), row),
        out_shape=jax.ShapeDtypeStruct((n, D_MODEL), F32),
        compiler_params=_params(("parallel",)), name="merge",
    )(x, oa, ob, oc, wg, bg, wb, wo, lg, lb)


def _memkv_kernel(m_ref, w_ref, kt_ref, v_ref):
    kv = _dot(m_ref[...].astype(BF16), w_ref[...])
    kt_ref[...] = kv[:, 0:X_WIDTH].T.astype(BF16)
    v_ref[...] = kv[:, X_WIDTH:2 * X_WIDTH].astype(BF16)


def _memkv(mem, wkv):
    batch, mlen, _ = mem.shape
    depth = wkv.shape[0]
    return pl.pallas_call(
        _memkv_kernel, grid=(depth, batch),
        in_specs=[pl.BlockSpec((None, mlen, D_MODEL), lambda l, b: (b, 0, 0)),
                  pl.BlockSpec((None, D_MODEL, 2 * X_WIDTH), lambda l, b: (l, 0, 0))],
        out_specs=(pl.BlockSpec((None, None, X_WIDTH, mlen), lambda l, b: (l, b, 0, 0)),
                   pl.BlockSpec((None, None, mlen, X_WIDTH), lambda l, b: (l, b, 0, 0))),
        out_shape=(jax.ShapeDtypeStruct((depth, batch, X_WIDTH, mlen), BF16),
                   jax.ShapeDtypeStruct((depth, batch, mlen, X_WIDTH), BF16)),
        compiler_params=_params(("parallel", "parallel")), name="mem_kv",
    )(mem, wkv)


def _xattn_kernel(alpha, x_ref, wq_ref, kt_ref, v_ref, wo_ref, g_ref, beta_ref, o_ref):
    x = x_ref[...]
    q = (_dot(x.astype(BF16), wq_ref[...]) * (1.0 / math.sqrt(X_HEAD_DIM))).astype(BF16)
    outs = []
    for h in range(X_HEADS):
        cols = slice(h * X_HEAD_DIM, (h + 1) * X_HEAD_DIM)
        s = _dot(q[:, cols], kt_ref[cols, :])
        m = jnp.max(s, axis=-1, keepdims=True)
        p = jnp.exp(s - m)
        denom = jnp.sum(p, axis=-1, keepdims=True)
        outs.append(_dot(p.astype(BF16), v_ref[:, cols]) / denom)
    att = jnp.concatenate(outs, axis=1).astype(BF16)
    h_out = _dot(att, wo_ref[...])
    o_ref[...] = _layer_norm(alpha * x + h_out, g_ref[...], beta_ref[...])


def _xattn(alpha, x, wq, kt, v, wo, lg, lb, layer, batch, seq, tm):
    n = x.shape[0]
    spb = seq // tm
    mlen = kt.shape[-1]
    lay3 = lambda b, s: (layer, 0, 0)
    in_specs = [
        pl.BlockSpec((tm, D_MODEL), lambda b, s: (b * spb + s, 0)),
        _resident((None, D_MODEL, X_WIDTH), lay3),
        pl.BlockSpec((None, None, X_WIDTH, mlen), lambda b, s: (layer, b, 0, 0)),
        pl.BlockSpec((None, None, mlen, X_WIDTH), lambda b, s: (layer, b, 0, 0)),
        _resident((None, X_WIDTH, D_MODEL), lay3),
        _resident((None, 1, D_MODEL), lay3), _resident((None, 1, D_MODEL), lay3),
    ]
    return pl.pallas_call(
        functools.partial(_xattn_kernel, alpha), grid=(batch, spb), in_specs=in_specs,
        out_specs=pl.BlockSpec((tm, D_MODEL), lambda b, s: (b * spb + s, 0)),
        out_shape=jax.ShapeDtypeStruct((n, D_MODEL), F32),
        compiler_params=_params(("parallel", "parallel")), name="cross_attn",
    )(x, wq, kt, v, wo, lg, lb)


def _ffn_kernel(alpha, xp_ref, x_ref, xn_ref, wu_ref, ck_ref, cb_ref, wd_ref, g_ref, beta_ref, o_ref,
                xe_ref, acc_ref):
    tm = x_ref.shape[0]
    step = pl.program_id(1)
    last = pl.num_programs(1) - 1
    x = x_ref[...]
    xe_ref[0:HALO, :] = jnp.where(step == 0, 0.0, xp_ref[...]).astype(BF16)
    xe_ref[HALO:HALO + tm, :] = x.astype(BF16)
    xe_ref[HALO + tm:2 * HALO + tm, :] = jnp.where(step == last, 0.0, xn_ref[...]).astype(BF16)
    acc_ref[...] = jnp.zeros_like(acc_ref)

    def body(c, carry):
        h = _dot(xe_ref[...], wu_ref[c])
        ck = ck_ref[c]
        conv = (h[HALO - 1:HALO - 1 + tm] * ck[0:1]
                + h[HALO:HALO + tm] * ck[1:2]
                + h[HALO + 1:HALO + 1 + tm] * ck[2:3]
                + cb_ref[c])
        act = (_gelu(conv[:, 0:FF_CHUNK]) * conv[:, FF_CHUNK:2 * FF_CHUNK]).astype(BF16)
        acc_ref[...] += _dot(act, wd_ref[c])
        return carry

    lax.fori_loop(0, wu_ref.shape[0], body, 0)
    o_ref[...] = _layer_norm(alpha * x + acc_ref[...], g_ref[...], beta_ref[...])


def _ffn(alpha, x, wu, ck, cb, wd, lg, lb, layer, batch, seq, tm):
    n = x.shape[0]
    spb = seq // tm
    hpt = tm // HALO
    nhalo = n // HALO
    nchunk = wu.shape[1]
    lay3 = lambda b, s: (layer, 0, 0)
    lay4 = lambda b, s: (layer, 0, 0, 0)
    prev = lambda b, s: (jnp.maximum((b * spb + s) * hpt - 1, 0), 0)
    nxt = lambda b, s: (jnp.minimum((b * spb + s + 1) * hpt, nhalo - 1), 0)
    in_specs = [
        pl.BlockSpec((HALO, D_MODEL), prev),
        pl.BlockSpec((tm, D_MODEL), lambda b, s: (b * spb + s, 0)),
        pl.BlockSpec((HALO, D_MODEL), nxt),
        _resident((None, nchunk, D_MODEL, 2 * FF_CHUNK), lay4),
        _resident((None, nchunk, CONV_W, 2 * FF_CHUNK), lay4),
        _resident((None, nchunk, 1, 2 * FF_CHUNK), lay4),
        _resident((None, nchunk, FF_CHUNK, D_MODEL), lay4),
        _resident((None, 1, D_MODEL), lay3), _resident((None, 1, D_MODEL), lay3),
    ]
    return pl.pallas_call(
        functools.partial(_ffn_kernel, alpha), grid=(batch, spb), in_specs=in_specs,
        out_specs=pl.BlockSpec((tm, D_MODEL), lambda b, s: (b * spb + s, 0)),
        out_shape=jax.ShapeDtypeStruct((n, D_MODEL), F32),
        scratch_shapes=[pltpu.VMEM((tm + 2 * HALO, D_MODEL), BF16), pltpu.VMEM((tm, D_MODEL), F32)],
        compiler_params=_params(("parallel", "parallel")), name="conv_ffn",
    )(x, x, x, wu, ck, cb, wd, lg, lb)


def _rope_tables(seq):
    def table(pos_parts, half):
        inv = ROPE_THETA ** (-jnp.arange(half, dtype=F32) * (1.0 / half))
        cos, sin = [], []
        for pos in pos_parts:
            ang = pos.astype(F32)[:, None] * inv[None, :]
            cos += [jnp.cos(ang), jnp.cos(ang)]
            sin += [-jnp.sin(ang), jnp.sin(ang)]
        cos = jnp.concatenate(cos, axis=1)
        sin = jnp.concatenate(sin, axis=1)
        reps = LANES // cos.shape[1]
        return jnp.tile(cos, (1, reps)), jnp.tile(sin, (1, reps))

    pos = jnp.arange(seq, dtype=jnp.int32)
    cosa, sina = table([pos], HEAD_DIM // 2)
    cosb, sinb = table([pos // GRID_W, pos % GRID_W], HEAD_DIM // 4)
    return cosa, sina, cosb, sinb


def kernel(x, mem, w_in, b_gate, a_sink, b_q_gain, b_k_gain, c_ln_g, c_ln_b, c_ws, c_bs, w_branch, w_mix_out,
           ln1_g, ln1_b, x_wq, x_wkv, x_wo, ln2_g, ln2_b, f_w_up, f_conv_k, f_conv_b, f_w_down, ln3_g, ln3_b):
    batch, seq, _ = x.shape
    depth = w_in.shape[0]
    n = batch * seq
    alpha = (2 * depth) ** 0.25
    tm = min(ROW_TILE, seq)
    assert seq % tm == 0 and tm % CHUNK == 0 and seq % GRID_W == 0
    assert D_FF % FF_CHUNK == 0
    nchunk = D_FF // FF_CHUNK

    w_main = w_in[:, :, 0:BRANCH_IN_W].astype(BF16)
    w_gate = w_in[:, :, BRANCH_IN_W:].astype(BF16)
    bg = b_gate.reshape(depth, 1, GATE_W)
    qg = jnp.tile(b_q_gain, (1, B_HEADS)).reshape(depth, 1, B_Q_W)
    kg = jnp.tile(b_k_gain, (1, B_KV)).reshape(depth, 1, B_KV_W)
    head_of = jnp.arange(B_Q_W, dtype=jnp.int32) // HEAD_DIM
    bd = (head_of[:, None] == head_of[None, :]).astype(BF16)
    lng = c_ln_g.reshape(depth, 1, C_WIDTH)
    lnb = c_ln_b.reshape(depth, 1, C_WIDTH)
    ws = c_ws.astype(BF16)
    bsb = jnp.broadcast_to(c_bs[:, :, :, None], (depth, C_GROUPS, CHUNK, C_WIDTH // C_GROUPS))
    wb = w_branch.astype(BF16)
    wo = w_mix_out.astype(BF16)
    wq = x_wq.astype(BF16)
    wkv = x_wkv.astype(BF16)
    xwo = x_wo.astype(BF16)

    def pair_chunks(a):
        lead = a.shape[:-1]
        a = a.reshape(lead + (2, nchunk, FF_CHUNK))
        a = jnp.moveaxis(a, -3, -2)
        return a.reshape(lead + (nchunk, 2 * FF_CHUNK))

    wu = jnp.moveaxis(pair_chunks(f_w_up.astype(BF16)), 2, 1)
    ck = jnp.moveaxis(pair_chunks(f_conv_k), 2, 1)
    cb = pair_chunks(f_conv_b).reshape(depth, nchunk, 1, 2 * FF_CHUNK)
    wd = f_w_down.astype(BF16).reshape(depth, nchunk, FF_CHUNK, D_MODEL)
    vec = lambda a: a.reshape(depth, 1, D_MODEL)
    l1g, l1b, l2g, l2b, l3g, l3b = map(vec, (ln1_g, ln1_b, ln2_g, ln2_b, ln3_g, ln3_b))
    tabs = _rope_tables(seq)

    mkt, mv = _memkv(mem, wkv)
    xs = x.reshape(n, D_MODEL)
    for l in range(depth):
        aq, akt, av, bq, bkt, bv, oc = _inproj(xs, w_main, tabs, qg, kg, bd, lng, lnb, ws, bsb, l, batch, seq, tm)
        oa = _attn_a(a_sink[l], aq, akt, av, l, batch, seq)
        ob = _attn_b(bq, bkt, bv, batch, seq)
        xs = _merge(alpha, xs, oa, ob, oc, w_gate, bg, wb, wo, l1g, l1b, l, tm)
        xs = _xattn(alpha, xs, wq, mkt, mv, xwo, l2g, l2b, l, batch, seq, tm)
        xs = _ffn(alpha, xs, wu, ck, cb, wd, l3g, l3b, l, batch, seq, tm)
    return xs.reshape(batch, seq, D_MODEL)
```

```python
import functools
import math

import jax
import jax.numpy as jnp
from jax import lax
from jax.experimental import pallas as pl
from jax.experimental.pallas import tpu as pltpu

F32 = jnp.float32
BF16 = jnp.bfloat16

D_MODEL = 1024
HEAD_DIM = 64
BLOCK = 128
A_HEADS = 8
A_KV = 2
WINDOW = 128
B_HEADS = 8
B_KV = 2
GRID_W = 64
C_WIDTH = 512
C_GROUPS = 4
CHUNK = 128
N_BRANCH = 3
ROPE_THETA = 10000.0
X_HEADS = 4
X_HEAD_DIM = 128
X_WIDTH = X_HEADS * X_HEAD_DIM
D_FF = 2816
CONV_W = 3
LN_EPS = 1e-5
RMS_EPS = 1e-6

A_Q_W = A_HEADS * HEAD_DIM
A_KV_W = A_KV * HEAD_DIM
B_Q_W = B_HEADS * HEAD_DIM
B_KV_W = B_KV * HEAD_DIM
QKV_W = A_Q_W + 2 * A_KV_W
BRANCH_IN_W = 2 * QKV_W + 2 * C_WIDTH
GATE_W = N_BRANCH * D_MODEL

LANES = 128
ROW_TILE = 512
ATTN_B_TQ = 512
ATTN_A_QBLOCKS = 4
FF_CHUNK = 256
BF16_SUBLANES = 16
HALO = BF16_SUBLANES
VMEM_LIMIT = 56 * 1024 * 1024
NEG_BIG = -1e30


def _dot(a, b):
    return jnp.dot(a, b, preferred_element_type=F32)


def _layer_norm(y, g, b):
    mu = jnp.mean(y, axis=-1, keepdims=True)
    d = y - mu
    var = jnp.mean(d * d, axis=-1, keepdims=True)
    return d * lax.rsqrt(var + LN_EPS) * g + b


def _gelu(x):
    return 0.5 * x * (1.0 + lax.erf(x * (1.0 / math.sqrt(2.0))))


def _resident(shape, index_map):
    return pl.BlockSpec(shape, index_map, pipeline_mode=pl.Buffered(1))


def _params(sem):
    return pltpu.CompilerParams(dimension_semantics=sem, vmem_limit_bytes=VMEM_LIMIT)


def _inproj_kernel(x_ref, w_ref, cosa_ref, sina_ref, cosb_ref, sinb_ref, qg_ref, kg_ref, bd_ref,
                   lng_ref, lnb_ref, ws_ref, bsb_ref,
                   aqt_ref, ak_ref, avt_ref, bqt_ref, bk_ref, bvt_ref, oc_ref):
    tm = x_ref.shape[0]
    xb = x_ref[...].astype(BF16)
    lane = lax.broadcasted_iota(jnp.int32, (tm, LANES), 1)

    pa = _dot(xb, w_ref[:, 0:QKV_W])
    cosa = cosa_ref[...]
    sina = sina_ref[...]
    first_a = (lane & (HEAD_DIM - 1)) < (HEAD_DIM // 2)

    def rope_a(seg):
        sw = jnp.where(first_a, pltpu.roll(seg, LANES - HEAD_DIM // 2, 1), pltpu.roll(seg, HEAD_DIM // 2, 1))
        return seg * cosa + sw * sina

    scale = 1.0 / math.sqrt(HEAD_DIM)
    for j in range(A_Q_W // LANES):
        sl = slice(j * LANES, (j + 1) * LANES)
        aqt_ref[sl, :] = (rope_a(pa[:, sl]) * scale).T.astype(BF16)
    akr = rope_a(pa[:, A_Q_W:A_Q_W + A_KV_W]).astype(BF16)
    for g in range(A_KV):
        ak_ref[g] = akr[:, g * HEAD_DIM:(g + 1) * HEAD_DIM]
    avt_ref[...] = pa[:, A_Q_W + A_KV_W:QKV_W].T.astype(BF16)

    pb = _dot(xb, w_ref[:, QKV_W:2 * QKV_W])
    cosb = cosb_ref[...]
    sinb = sinb_ref[...]
    quarter = HEAD_DIM // 4
    first_b = (lane & (2 * quarter - 1)) < quarter

    def rope_b(seg):
        sw = jnp.where(first_b, pltpu.roll(seg, LANES - quarter, 1), pltpu.roll(seg, quarter, 1))
        return seg * cosb + sw * sinb

    def head_sumsq(v, bd):
        sq = v * v
        hi = sq.astype(BF16)
        lo = (sq - hi.astype(F32)).astype(BF16)
        return _dot(hi, bd) + _dot(lo, bd)

    bq = pb[:, 0:B_Q_W]
    bqn = bq * lax.rsqrt(head_sumsq(bq, bd_ref[...]) * (1.0 / HEAD_DIM) + RMS_EPS) * qg_ref[...]
    for j in range(B_Q_W // LANES):
        sl = slice(j * LANES, (j + 1) * LANES)
        bqt_ref[sl, :] = (rope_b(bqn[:, sl]) * scale).T.astype(BF16)
    bk = pb[:, B_Q_W:B_Q_W + B_KV_W]
    bkn = bk * lax.rsqrt(head_sumsq(bk, bd_ref[0:B_KV_W, 0:B_KV_W]) * (1.0 / HEAD_DIM) + RMS_EPS) * kg_ref[...]
    bkr = rope_b(bkn).astype(BF16)
    for g in range(B_KV):
        bk_ref[g] = bkr[:, g * HEAD_DIM:(g + 1) * HEAD_DIM]
    bvt_ref[...] = pb[:, B_Q_W + B_KV_W:QKV_W].T.astype(BF16)

    pc = _gelu(_dot(xb, w_ref[:, 2 * QKV_W:BRANCH_IN_W]))
    u = pc[:, 0:C_WIDTH]
    vln = _layer_norm(pc[:, C_WIDTH:2 * C_WIDTH], lng_ref[...], lnb_ref[...]).astype(BF16)
    nch = tm // CHUNK
    gw = C_WIDTH // C_GROUPS
    for g in range(C_GROUPS):
        cs = slice(g * gw, (g + 1) * gw)
        rhs = jnp.concatenate([vln[r * CHUNK:(r + 1) * CHUNK, cs] for r in range(nch)], axis=1)
        mixed = _dot(ws_ref[g], rhs)
        for r in range(nch):
            rs = slice(r * CHUNK, (r + 1) * CHUNK)
            oc_ref[rs, cs] = (u[rs, cs] * (mixed[:, r * gw:(r + 1) * gw] + bsb_ref[g])).astype(BF16)


def _inproj(x, w, tabs, qg, kg, bd, lng, lnb, ws, bsb, layer, batch, seq, tm):
    n = x.shape[0]
    spb = seq // tm
    row = lambda i: (i, 0)
    tab = lambda i: (i % spb, 0)
    lay3 = lambda i: (layer, 0, 0)
    lay4 = lambda i: (layer, 0, 0, 0)
    tr = lambda i: (i // spb, 0, i % spb)
    out_shape = (
        jax.ShapeDtypeStruct((batch, A_Q_W, seq), BF16),
        jax.ShapeDtypeStruct((A_KV, n, HEAD_DIM), BF16),
        jax.ShapeDtypeStruct((batch, A_KV_W, seq), BF16),
        jax.ShapeDtypeStruct((batch, B_Q_W, seq), BF16),
        jax.ShapeDtypeStruct((B_KV, n, HEAD_DIM), BF16),
        jax.ShapeDtypeStruct((batch, B_KV_W, seq), BF16),
        jax.ShapeDtypeStruct((n, C_WIDTH), BF16),
    )
    in_specs = [
        pl.BlockSpec((tm, D_MODEL), row),
        _resident((None, D_MODEL, BRANCH_IN_W), lay3),
        pl.BlockSpec((tm, LANES), tab), pl.BlockSpec((tm, LANES), tab),
        pl.BlockSpec((tm, LANES), tab), pl.BlockSpec((tm, LANES), tab),
        _resident((None, 1, B_Q_W), lay3), _resident((None, 1, B_KV_W), lay3),
        _resident((B_Q_W, B_Q_W), lambda i: (0, 0)),
        _resident((None, 1, C_WIDTH), lay3), _resident((None, 1, C_WIDTH), lay3),
        _resident((None, C_GROUPS, CHUNK, CHUNK), lay4),
        _resident((None, C_GROUPS, CHUNK, C_WIDTH // C_GROUPS), lay4),
    ]
    out_specs = (
        pl.BlockSpec((None, A_Q_W, tm), tr),
        pl.BlockSpec((A_KV, tm, HEAD_DIM), lambda i: (0, i, 0)),
        pl.BlockSpec((None, A_KV_W, tm), tr),
        pl.BlockSpec((None, B_Q_W, tm), tr),
        pl.BlockSpec((B_KV, tm, HEAD_DIM), lambda i: (0, i, 0)),
        pl.BlockSpec((None, B_KV_W, tm), tr),
        pl.BlockSpec((tm, C_WIDTH), row),
    )
    return pl.pallas_call(
        _inproj_kernel, grid=(n // tm,), in_specs=in_specs, out_specs=out_specs, out_shape=out_shape,
        compiler_params=_params(("parallel",)), name="inproj",
    )(x, w, *tabs, qg, kg, bd, lng, lnb, ws, bsb)


def _attn_a_kernel(sink_ref, qt_ref, kl_ref, kc_ref, kr_ref, vtl_ref, vtc_ref, vtr_ref, ones_ref, o_ref):
    qblocks = qt_ref.shape[1] // BLOCK
    group = A_HEADS // A_KV
    step = pl.program_id(1)
    last = pl.num_programs(1) - 1
    span = 3 * BLOCK
    kj = lax.broadcasted_iota(jnp.int32, (span, group * BLOCK), 0)
    qi = lax.broadcasted_iota(jnp.int32, (span, group * BLOCK), 1) & (BLOCK - 1)
    band = (kj >= qi) & (kj <= qi + 2 * WINDOW)
    units = [(g, t) for g in range(A_KV) for t in range(qblocks)]

    scores = []
    for g, t in units:
        kspan = jnp.concatenate([kl_ref[g], kc_ref[g], kr_ref[g]], axis=0)
        qt = jnp.concatenate(
            [qt_ref[(g * group + h) * HEAD_DIM:(g * group + h + 1) * HEAD_DIM, t * BLOCK:(t + 1) * BLOCK]
             for h in range(group)], axis=1)
        scores.append(_dot(kspan[t * BLOCK:t * BLOCK + span], qt))
    probs = []
    for (g, t), st in zip(units, scores):
        valid = band
        if t == 0:
            valid = valid & (kj >= jnp.where(step == 0, BLOCK, 0))
        if t == qblocks - 1:
            valid = valid & (kj < jnp.where(step == last, 2 * BLOCK, span))
        st = jnp.where(valid, st, NEG_BIG)
        m = jnp.maximum(jnp.max(st, axis=0, keepdims=True), sink_ref[g])
        probs.append((jnp.exp(st - m).astype(BF16), jnp.exp(sink_ref[g] - m)))
    for (g, t), (pt, sink_p) in zip(units, probs):
        rows = slice(g * HEAD_DIM, (g + 1) * HEAD_DIM)
        vts = jnp.concatenate([vtl_ref[rows, :], vtc_ref[rows, :], vtr_ref[rows, :]], axis=1)
        vte = jnp.concatenate([vts, ones_ref[...]], axis=0)
        oe = _dot(vte[:, t * BLOCK:t * BLOCK + span], pt)
        o = oe[0:HEAD_DIM] / (oe[HEAD_DIM:HEAD_DIM + 1] + sink_p)
        stacked = jnp.concatenate([o[:, h * BLOCK:(h + 1) * BLOCK] for h in range(group)], axis=0)
        o_ref[t * BLOCK:(t + 1) * BLOCK, g * group * HEAD_DIM:(g + 1) * group * HEAD_DIM] = (
            stacked.T.astype(BF16))


def _attn_a(sink_rows, qt, k, vt, batch, seq):
    n = batch * seq
    qb = ATTN_A_QBLOCKS if seq % (ATTN_A_QBLOCKS * BLOCK) == 0 else 1
    tq = qb * BLOCK
    steps = seq // tq
    nblk = seq // BLOCK
    left = lambda s: jnp.maximum(s * qb - 1, 0)
    right = lambda s: jnp.minimum((s + 1) * qb, nblk - 1)
    ones = (lax.broadcasted_iota(jnp.int32, (BF16_SUBLANES, tq + 2 * BLOCK), 0) == 0).astype(BF16)
    in_specs = [
        _resident(sink_rows.shape, lambda b, s: (0, 0, 0)),
        pl.BlockSpec((None, A_Q_W, tq), lambda b, s: (b, 0, s)),
        pl.BlockSpec((A_KV, BLOCK, HEAD_DIM), lambda b, s: (0, b * nblk + left(s), 0)),
        pl.BlockSpec((A_KV, tq, HEAD_DIM), lambda b, s: (0, b * steps + s, 0)),
        pl.BlockSpec((A_KV, BLOCK, HEAD_DIM), lambda b, s: (0, b * nblk + right(s), 0)),
        pl.BlockSpec((None, A_KV_W, BLOCK), lambda b, s: (b, 0, left(s))),
        pl.BlockSpec((None, A_KV_W, tq), lambda b, s: (b, 0, s)),
        pl.BlockSpec((None, A_KV_W, BLOCK), lambda b, s: (b, 0, right(s))),
        _resident(ones.shape, lambda b, s: (0, 0)),
    ]
    return pl.pallas_call(
        _attn_a_kernel, grid=(batch, steps), in_specs=in_specs,
        out_specs=pl.BlockSpec((tq, A_Q_W), lambda b, s: (b * steps + s, 0)),
        out_shape=jax.ShapeDtypeStruct((n, A_Q_W), BF16),
        compiler_params=_params(("parallel", "parallel")), name="attn_window",
    )(sink_rows, qt, k, k, k, vt, vt, vt, ones)


def _attn_b_kernel(qt_ref, k_ref, vt_ref, ones_ref, o_ref):
    k = k_ref[...]
    vte = jnp.concatenate([vt_ref[...], ones_ref[...]], axis=0)
    group = B_HEADS // B_KV

    def scores(h):
        return _dot(k, qt_ref[h * HEAD_DIM:(h + 1) * HEAD_DIM, :])

    outs = []
    st_next = scores(0)
    for h in range(group):
        st = st_next
        if h + 1 < group:
            st_next = scores(h + 1)
        m = jnp.max(st, axis=0, keepdims=True)
        pt = jnp.exp(st - m).astype(BF16)
        oe = _dot(vte, pt)
        outs.append(oe[0:HEAD_DIM] / oe[HEAD_DIM:HEAD_DIM + 1])
    o_ref[...] = jnp.concatenate(outs, axis=0).T.astype(BF16)


def _attn_b(qt, k, vt, batch, seq):
    n = batch * seq
    tq = min(ATTN_B_TQ, seq)
    steps = seq // tq
    gw = B_Q_W // B_KV
    ones = (lax.broadcasted_iota(jnp.int32, (BF16_SUBLANES, seq), 0) == 0).astype(BF16)
    in_specs = [
        pl.BlockSpec((None, gw, tq), lambda b, g, s: (b, g, s)),
        pl.BlockSpec((None, seq, HEAD_DIM), lambda b, g, s: (g, b, 0)),
        pl.BlockSpec((None, HEAD_DIM, seq), lambda b, g, s: (b, g, 0)),
        _resident((BF16_SUBLANES, seq), lambda b, g, s: (0, 0)),
    ]
    return pl.pallas_call(
        _attn_b_kernel, grid=(batch, B_KV, steps), in_specs=in_specs,
        out_specs=pl.BlockSpec((tq, gw), lambda b, g, s: (b * steps + s, g)),
        out_shape=jax.ShapeDtypeStruct((n, B_Q_W), BF16),
        compiler_params=_params(("parallel", "parallel", "parallel")), name="attn_dense",
    )(qt, k, vt, ones)


def _merge_kernel(alpha, x_ref, a_ref, b_ref, c_ref, wg_ref, bg_ref, wb_ref, wo_ref, g_ref, beta_ref, o_ref):
    x = x_ref[...]
    xb = x.astype(BF16)
    merged = None
    for i, br in enumerate((a_ref, b_ref, c_ref)):
        cols = slice(i * D_MODEL, (i + 1) * D_MODEL)
        gate = jax.nn.sigmoid(_dot(xb, wg_ref[:, cols]) + bg_ref[:, cols])
        t = _dot(br[...], wb_ref[i]) * gate
        merged = t if merged is None else merged + t
    h = _dot(merged.astype(BF16), wo_ref[...])
    o_ref[...] = _layer_norm(alpha * x + h, g_ref[...], beta_ref[...])


def _merge(alpha, x, oa, ob, oc, wg, bg, wb, wo, lg, lb, layer, tm):
    n = x.shape[0]
    row = lambda i: (i, 0)
    lay3 = lambda i: (layer, 0, 0)
    lay4 = lambda i: (layer, 0, 0, 0)
    in_specs = [
        pl.BlockSpec((tm, D_MODEL), row),
        pl.BlockSpec((tm, A_Q_W), row), pl.BlockSpec((tm, B_Q_W), row), pl.BlockSpec((tm, C_WIDTH), row),
        _resident((None, D_MODEL, GATE_W), lay3),
        _resident((None, 1, GATE_W), lay3),
        _resident((None, N_BRANCH, C_WIDTH, D_MODEL), lay4),
        _resident((None, D_MODEL, D_MODEL), lay3),
        _resident((None, 1, D_MODEL), lay3), _resident((None, 1, D_MODEL), lay3),
    ]
    return pl.pallas_call(
        functools.partial(_merge_kernel, alpha), grid=(n // tm,), in_specs=in_specs,
        out_specs=pl.BlockSpec((tm, D_MODEL), row),
        out_shape=jax.ShapeDtypeStruct((n, D_MODEL), F32),
        compiler_params=_params(("parallel",)), name="merge",
    )(x, oa, ob, oc, wg, bg, wb, wo, lg, lb)


def _memkv_kernel(m_ref, w_ref, kt_ref, v_ref):
    kv = _dot(m_ref[...].astype(BF16), w_ref[...])
    kt_ref[...] = kv[:, 0:X_WIDTH].T.astype(BF16)
    v_ref[...] = kv[:, X_WIDTH:2 * X_WIDTH].astype(BF16)


def _memkv(mem, wkv):
    batch, mlen, _ = mem.shape
    depth = wkv.shape[0]
    return pl.pallas_call(
        _memkv_kernel, grid=(depth, batch),
        in_specs=[pl.BlockSpec((None, mlen, D_MODEL), lambda l, b: (b, 0, 0)),
                  pl.BlockSpec((None, D_MODEL, 2 * X_WIDTH), lambda l, b: (l, 0, 0))],
        out_specs=(pl.BlockSpec((None, None, X_WIDTH, mlen), lambda l, b: (l, b, 0, 0)),
                   pl.BlockSpec((None, None, mlen, X_WIDTH), lambda l, b: (l, b, 0, 0))),
        out_shape=(jax.ShapeDtypeStruct((depth, batch, X_WIDTH, mlen), BF16),
                   jax.ShapeDtypeStruct((depth, batch, mlen, X_WIDTH), BF16)),
        compiler_params=_params(("parallel", "parallel")), name="mem_kv",
    )(mem, wkv)


def _xattn_kernel(alpha, x_ref, wq_ref, kt_ref, v_ref, wo_ref, g_ref, beta_ref, o_ref):
    x = x_ref[...]
    q = (_dot(x.astype(BF16), wq_ref[...]) * (1.0 / math.sqrt(X_HEAD_DIM))).astype(BF16)
    outs = []
    for h in range(X_HEADS):
        cols = slice(h * X_HEAD_DIM, (h + 1) * X_HEAD_DIM)
        s = _dot(q[:, cols], kt_ref[cols, :])
        m = jnp.max(s, axis=-1, keepdims=True)
        p = jnp.exp(s - m)
        denom = jnp.sum(p, axis=-1, keepdims=True)
        outs.append(_dot(p.astype(BF16), v_ref[:, cols]) / denom)
    att = jnp.concatenate(outs, axis=1).astype(BF16)
    h_out = _dot(att, wo_ref[...])
    o_ref[...] = _layer_norm(alpha * x + h_out, g_ref[...], beta_ref[...])


def _xattn(alpha, x, wq, kt, v, wo, lg, lb, layer, batch, seq, tm):
    n = x.shape[0]
    spb = seq // tm
    mlen = kt.shape[-1]
    lay3 = lambda b, s: (layer, 0, 0)
    in_specs = [
        pl.BlockSpec((tm, D_MODEL), lambda b, s: (b * spb + s, 0)),
        _resident((None, D_MODEL, X_WIDTH), lay3),
        pl.BlockSpec((None, None, X_WIDTH, mlen), lambda b, s: (layer, b, 0, 0)),
        pl.BlockSpec((None, None, mlen, X_WIDTH), lambda b, s: (layer, b, 0, 0)),
        _resident((None, X_WIDTH, D_MODEL), lay3),
        _resident((None, 1, D_MODEL), lay3), _resident((None, 1, D_MODEL), lay3),
    ]
    return pl.pallas_call(
        functools.partial(_xattn_kernel, alpha), grid=(batch, spb), in_specs=in_specs,
        out_specs=pl.BlockSpec((tm, D_MODEL), lambda b, s: (b * spb + s, 0)),
        out_shape=jax.ShapeDtypeStruct((n, D_MODEL), F32),
        compiler_params=_params(("parallel", "parallel")), name="cross_attn",
    )(x, wq, kt, v, wo, lg, lb)


def _ffn_kernel(alpha, xp_ref, x_ref, xn_ref, wu_ref, ck_ref, cb_ref, wd_ref, g_ref, beta_ref, o_ref,
                xe_ref, h_ref):
    tm = x_ref.shape[0]
    step = pl.program_id(1)
    last = pl.num_programs(1) - 1
    x = x_ref[...]
    xe_ref[0:HALO, :] = jnp.where(step == 0, 0.0, xp_ref[...]).astype(BF16)
    xe_ref[HALO:HALO + tm, :] = x.astype(BF16)
    xe_ref[HALO + tm:2 * HALO + tm, :] = jnp.where(step == last, 0.0, xn_ref[...]).astype(BF16)
    nchunk = wu_ref.shape[0]

    def up(c):
        h_ref[c % 2] = _dot(xe_ref[...], wu_ref[c])

    def gate(c):
        ck = ck_ref[c]
        hs = h_ref.at[c % 2]
        conv = (hs[HALO - 1:HALO - 1 + tm, :] * ck[0:1]
                + hs[HALO:HALO + tm, :] * ck[1:2]
                + hs[HALO + 1:HALO + 1 + tm, :] * ck[2:3]
                + cb_ref[c])
        return (_gelu(conv[:, 0:FF_CHUNK]) * conv[:, FF_CHUNK:2 * FF_CHUNK]).astype(BF16)

    up(0)
    acc = None
    for c in range(nchunk):
        if c + 1 < nchunk:
            up(c + 1)
        d = _dot(gate(c), wd_ref[c])
        acc = d if acc is None else acc + d
    o_ref[...] = _layer_norm(alpha * x + acc, g_ref[...], beta_ref[...])


def _ffn(alpha, x, wu, ck, cb, wd, lg, lb, layer, batch, seq, tm):
    n = x.shape[0]
    spb = seq // tm
    hpt = tm // HALO
    nhalo = n // HALO
    nchunk = wu.shape[1]
    lay3 = lambda b, s: (layer, 0, 0)
    lay4 = lambda b, s: (layer, 0, 0, 0)
    prev = lambda b, s: (jnp.maximum((b * spb + s) * hpt - 1, 0), 0)
    nxt = lambda b, s: (jnp.minimum((b * spb + s + 1) * hpt, nhalo - 1), 0)
    in_specs = [
        pl.BlockSpec((HALO, D_MODEL), prev),
        pl.BlockSpec((tm, D_MODEL), lambda b, s: (b * spb + s, 0)),
        pl.BlockSpec((HALO, D_MODEL), nxt),
        _resident((None, nchunk, D_MODEL, 2 * FF_CHUNK), lay4),
        _resident((None, nchunk, CONV_W, 2 * FF_CHUNK), lay4),
        _resident((None, nchunk, 1, 2 * FF_CHUNK), lay4),
        _resident((None, nchunk, FF_CHUNK, D_MODEL), lay4),
        _resident((None, 1, D_MODEL), lay3), _resident((None, 1, D_MODEL), lay3),
    ]
    return pl.pallas_call(
        functools.partial(_ffn_kernel, alpha), grid=(batch, spb), in_specs=in_specs,
        out_specs=pl.BlockSpec((tm, D_MODEL), lambda b, s: (b * spb + s, 0)),
        out_shape=jax.ShapeDtypeStruct((n, D_MODEL), F32),
        scratch_shapes=[pltpu.VMEM((tm + 2 * HALO, D_MODEL), BF16),
                        pltpu.VMEM((2, tm + 2 * HALO, 2 * FF_CHUNK), F32)],
        compiler_params=_params(("parallel", "parallel")), name="conv_ffn",
    )(x, x, x, wu, ck, cb, wd, lg, lb)


def _rope_tables(seq):
    def table(pos_parts, half):
        inv = ROPE_THETA ** (-jnp.arange(half, dtype=F32) * (1.0 / half))
        cos, sin = [], []
        for pos in pos_parts:
            ang = pos.astype(F32)[:, None] * inv[None, :]
            cos += [jnp.cos(ang), jnp.cos(ang)]
            sin += [-jnp.sin(ang), jnp.sin(ang)]
        cos = jnp.concatenate(cos, axis=1)
        sin = jnp.concatenate(sin, axis=1)
        reps = LANES // cos.shape[1]
        return jnp.tile(cos, (1, reps)), jnp.tile(sin, (1, reps))

    pos = jnp.arange(seq, dtype=jnp.int32)
    cosa, sina = table([pos], HEAD_DIM // 2)
    cosb, sinb = table([pos // GRID_W, pos % GRID_W], HEAD_DIM // 4)
    return cosa, sina, cosb, sinb


def kernel(x, mem, w_in, b_gate, a_sink, b_q_gain, b_k_gain, c_ln_g, c_ln_b, c_ws, c_bs, w_branch, w_mix_out,
           ln1_g, ln1_b, x_wq, x_wkv, x_wo, ln2_g, ln2_b, f_w_up, f_conv_k, f_conv_b, f_w_down, ln3_g, ln3_b):
    batch, seq, _ = x.shape
    depth = w_in.shape[0]
    n = batch * seq
    alpha = (2 * depth) ** 0.25
    tm = min(ROW_TILE, seq)
    assert seq % tm == 0 and tm % CHUNK == 0 and seq % GRID_W == 0
    assert D_FF % FF_CHUNK == 0
    nchunk = D_FF // FF_CHUNK

    w_main = w_in[:, :, 0:BRANCH_IN_W].astype(BF16)
    w_gate = w_in[:, :, BRANCH_IN_W:].astype(BF16)
    bg = b_gate.reshape(depth, 1, GATE_W)
    qg = jnp.tile(b_q_gain, (1, B_HEADS)).reshape(depth, 1, B_Q_W)
    kg = jnp.tile(b_k_gain, (1, B_KV)).reshape(depth, 1, B_KV_W)
    head_of = jnp.arange(B_Q_W, dtype=jnp.int32) // HEAD_DIM
    bd = (head_of[:, None] == head_of[None, :]).astype(BF16)
    lng = c_ln_g.reshape(depth, 1, C_WIDTH)
    lnb = c_ln_b.reshape(depth, 1, C_WIDTH)
    ws = c_ws.astype(BF16)
    bsb = jnp.broadcast_to(c_bs[:, :, :, None], (depth, C_GROUPS, CHUNK, C_WIDTH // C_GROUPS))
    wb = w_branch.astype(BF16)
    wo = w_mix_out.astype(BF16)
    wq = x_wq.astype(BF16)
    wkv = x_wkv.astype(BF16)
    xwo = x_wo.astype(BF16)

    def pair_chunks(a):
        lead = a.shape[:-1]
        a = a.reshape(lead + (2, nchunk, FF_CHUNK))
        a = jnp.moveaxis(a, -3, -2)
        return a.reshape(lead + (nchunk, 2 * FF_CHUNK))

    wu = jnp.moveaxis(pair_chunks(f_w_up.astype(BF16)), 2, 1)
    ck = jnp.moveaxis(pair_chunks(f_conv_k), 2, 1)
    cb = pair_chunks(f_conv_b).reshape(depth, nchunk, 1, 2 * FF_CHUNK)
    wd = f_w_down.astype(BF16).reshape(depth, nchunk, FF_CHUNK, D_MODEL)
    vec = lambda a: a.reshape(depth, 1, D_MODEL)
    l1g, l1b, l2g, l2b, l3g, l3b = map(vec, (ln1_g, ln1_b, ln2_g, ln2_b, ln3_g, ln3_b))
    tabs = _rope_tables(seq)
    sink_rows = jnp.repeat(a_sink, BLOCK, axis=1).reshape(depth, A_KV, 1, (A_HEADS // A_KV) * BLOCK)

    mkt, mv = _memkv(mem, wkv)
    xs = x.reshape(n, D_MODEL)
    for l in range(depth):
        aqt, ak, avt, bqt, bk, bvt, oc = _inproj(xs, w_main, tabs, qg, kg, bd, lng, lnb, ws, bsb, l, batch, seq, tm)
        oa = _attn_a(sink_rows[l], aqt, ak, avt, batch, seq)
        ob = _attn_b(bqt, bk, bvt, batch, seq)
        xs = _merge(alpha, xs, oa, ob, oc, w_gate, bg, wb, wo, l1g, l1b, l, tm)
        xs = _xattn(alpha, xs, wq, mkt, mv, xwo, l2g, l2b, l, batch, seq, tm)
        xs = _ffn(alpha, xs, wu, ck, cb, wd, l3g, l3b, l, batch, seq, tm)
    return xs.reshape(batch, seq, D_MODEL)
```

```python
import functools
import math

import jax
import jax.numpy as jnp
from jax import lax
from jax.experimental import pallas as pl
from jax.experimental.pallas import tpu as pltpu

F32 = jnp.float32
BF16 = jnp.bfloat16

D_MODEL = 1024
HEAD_DIM = 64
BLOCK = 128
A_HEADS = 8
A_KV = 2
WINDOW = 128
B_HEADS = 8
B_KV = 2
GRID_W = 64
C_WIDTH = 512
C_GROUPS = 4
CHUNK = 128
N_BRANCH = 3
ROPE_THETA = 10000.0
X_HEADS = 4
X_HEAD_DIM = 128
X_WIDTH = X_HEADS * X_HEAD_DIM
D_FF = 2816
CONV_W = 3
LN_EPS = 1e-5
RMS_EPS = 1e-6

A_Q_W = A_HEADS * HEAD_DIM
A_KV_W = A_KV * HEAD_DIM
B_Q_W = B_HEADS * HEAD_DIM
B_KV_W = B_KV * HEAD_DIM
QKV_W = A_Q_W + 2 * A_KV_W
BRANCH_IN_W = 2 * QKV_W + 2 * C_WIDTH
GATE_W = N_BRANCH * D_MODEL

LANES = 128
ROW_TILE = 1024
FFN_ROW_TILE = 512
ATTN_B_TQ = 512
ATTN_A_QBLOCKS = 4
FF_CHUNK = 256
BF16_SUBLANES = 16
HALO = BF16_SUBLANES
VMEM_LIMIT = 56 * 1024 * 1024
NEG_BIG = -1e30


def _dot(a, b):
    return jnp.dot(a, b, preferred_element_type=F32)


def _layer_norm(y, g, b):
    mu = jnp.mean(y, axis=-1, keepdims=True)
    d = y - mu
    var = jnp.mean(d * d, axis=-1, keepdims=True)
    return d * lax.rsqrt(var + LN_EPS) * g + b


def _gelu(x):
    return 0.5 * x * (1.0 + lax.erf(x * (1.0 / math.sqrt(2.0))))


def _resident(shape, index_map):
    return pl.BlockSpec(shape, index_map, pipeline_mode=pl.Buffered(1))


def _params(sem):
    return pltpu.CompilerParams(dimension_semantics=sem, vmem_limit_bytes=VMEM_LIMIT)


def _inproj_kernel(x_ref, w_ref, cosa_ref, sina_ref, cosb_ref, sinb_ref, qg_ref, kg_ref, bd_ref,
                   lng_ref, lnb_ref, ws_ref, bsb_ref,
                   aqt_ref, ak_ref, avt_ref, bqt_ref, bk_ref, bvt_ref, oc_ref):
    tm = x_ref.shape[0]
    xb = x_ref[...].astype(BF16)
    lane = lax.broadcasted_iota(jnp.int32, (tm, LANES), 1)

    pa = _dot(xb, w_ref[:, 0:QKV_W])
    cosa = cosa_ref[...]
    sina = sina_ref[...]
    first_a = (lane & (HEAD_DIM - 1)) < (HEAD_DIM // 2)

    def rope_a(seg):
        sw = jnp.where(first_a, pltpu.roll(seg, LANES - HEAD_DIM // 2, 1), pltpu.roll(seg, HEAD_DIM // 2, 1))
        return seg * cosa + sw * sina

    scale = 1.0 / math.sqrt(HEAD_DIM)
    for j in range(A_Q_W // LANES):
        sl = slice(j * LANES, (j + 1) * LANES)
        aqt_ref[sl, :] = (rope_a(pa[:, sl]) * scale).T.astype(BF16)
    akr = rope_a(pa[:, A_Q_W:A_Q_W + A_KV_W]).astype(BF16)
    for g in range(A_KV):
        ak_ref[g] = akr[:, g * HEAD_DIM:(g + 1) * HEAD_DIM]
    avt_ref[...] = pa[:, A_Q_W + A_KV_W:QKV_W].T.astype(BF16)

    pb = _dot(xb, w_ref[:, QKV_W:2 * QKV_W])
    cosb = cosb_ref[...]
    sinb = sinb_ref[...]
    quarter = HEAD_DIM // 4
    first_b = (lane & (2 * quarter - 1)) < quarter

    def rope_b(seg):
        sw = jnp.where(first_b, pltpu.roll(seg, LANES - quarter, 1), pltpu.roll(seg, quarter, 1))
        return seg * cosb + sw * sinb

    def head_sumsq(v, bd):
        sq = v * v
        hi = sq.astype(BF16)
        lo = (sq - hi.astype(F32)).astype(BF16)
        return _dot(hi, bd) + _dot(lo, bd)

    bq = pb[:, 0:B_Q_W]
    bqn = bq * lax.rsqrt(head_sumsq(bq, bd_ref[...]) * (1.0 / HEAD_DIM) + RMS_EPS) * qg_ref[...]
    for j in range(B_Q_W // LANES):
        sl = slice(j * LANES, (j + 1) * LANES)
        bqt_ref[sl, :] = (rope_b(bqn[:, sl]) * scale).T.astype(BF16)
    bk = pb[:, B_Q_W:B_Q_W + B_KV_W]
    bkn = bk * lax.rsqrt(head_sumsq(bk, bd_ref[0:B_KV_W, 0:B_KV_W]) * (1.0 / HEAD_DIM) + RMS_EPS) * kg_ref[...]
    bkr = rope_b(bkn).astype(BF16)
    for g in range(B_KV):
        bk_ref[g] = bkr[:, g * HEAD_DIM:(g + 1) * HEAD_DIM]
    bvt_ref[...] = pb[:, B_Q_W + B_KV_W:QKV_W].T.astype(BF16)

    pc = _gelu(_dot(xb, w_ref[:, 2 * QKV_W:BRANCH_IN_W]))
    u = pc[:, 0:C_WIDTH]
    vln = _layer_norm(pc[:, C_WIDTH:2 * C_WIDTH], lng_ref[...], lnb_ref[...]).astype(BF16)
    nch = tm // CHUNK
    gw = C_WIDTH // C_GROUPS
    for g in range(C_GROUPS):
        cs = slice(g * gw, (g + 1) * gw)
        rhs = jnp.concatenate([vln[r * CHUNK:(r + 1) * CHUNK, cs] for r in range(nch)], axis=1)
        mixed = _dot(ws_ref[g], rhs)
        for r in range(nch):
            rs = slice(r * CHUNK, (r + 1) * CHUNK)
            oc_ref[rs, cs] = (u[rs, cs] * (mixed[:, r * gw:(r + 1) * gw] + bsb_ref[g])).astype(BF16)


def _inproj(x, w, tabs, qg, kg, bd, lng, lnb, ws, bsb, layer, batch, seq, tm):
    n = x.shape[0]
    spb = seq // tm
    row = lambda i: (i, 0)
    tab = lambda i: (i % spb, 0)
    lay3 = lambda i: (layer, 0, 0)
    lay4 = lambda i: (layer, 0, 0, 0)
    tr = lambda i: (i // spb, 0, i % spb)
    out_shape = (
        jax.ShapeDtypeStruct((batch, A_Q_W, seq), BF16),
        jax.ShapeDtypeStruct((A_KV, n, HEAD_DIM), BF16),
        jax.ShapeDtypeStruct((batch, A_KV_W, seq), BF16),
        jax.ShapeDtypeStruct((batch, B_Q_W, seq), BF16),
        jax.ShapeDtypeStruct((B_KV, n, HEAD_DIM), BF16),
        jax.ShapeDtypeStruct((batch, B_KV_W, seq), BF16),
        jax.ShapeDtypeStruct((n, C_WIDTH), BF16),
    )
    in_specs = [
        pl.BlockSpec((tm, D_MODEL), row),
        _resident((None, D_MODEL, BRANCH_IN_W), lay3),
        pl.BlockSpec((tm, LANES), tab), pl.BlockSpec((tm, LANES), tab),
        pl.BlockSpec((tm, LANES), tab), pl.BlockSpec((tm, LANES), tab),
        _resident((None, 1, B_Q_W), lay3), _resident((None, 1, B_KV_W), lay3),
        _resident((B_Q_W, B_Q_W), lambda i: (0, 0)),
        _resident((None, 1, C_WIDTH), lay3), _resident((None, 1, C_WIDTH), lay3),
        _resident((None, C_GROUPS, CHUNK, CHUNK), lay4),
        _resident((None, C_GROUPS, CHUNK, C_WIDTH // C_GROUPS), lay4),
    ]
    out_specs = (
        pl.BlockSpec((None, A_Q_W, tm), tr),
        pl.BlockSpec((A_KV, tm, HEAD_DIM), lambda i: (0, i, 0)),
        pl.BlockSpec((None, A_KV_W, tm), tr),
        pl.BlockSpec((None, B_Q_W, tm), tr),
        pl.BlockSpec((B_KV, tm, HEAD_DIM), lambda i: (0, i, 0)),
        pl.BlockSpec((None, B_KV_W, tm), tr),
        pl.BlockSpec((tm, C_WIDTH), row),
    )
    return pl.pallas_call(
        _inproj_kernel, grid=(n // tm,), in_specs=in_specs, out_specs=out_specs, out_shape=out_shape,
        compiler_params=_params(("parallel",)), name="inproj",
    )(x, w, *tabs, qg, kg, bd, lng, lnb, ws, bsb)


def _attn_a_kernel(sink_ref, qt_ref, kl_ref, kc_ref, kr_ref, vtl_ref, vtc_ref, vtr_ref, ones_ref, o_ref):
    qblocks = qt_ref.shape[1] // BLOCK
    group = A_HEADS // A_KV
    step = pl.program_id(1)
    last = pl.num_programs(1) - 1
    span = 3 * BLOCK
    kj = lax.broadcasted_iota(jnp.int32, (span, group * BLOCK), 0)
    qi = lax.broadcasted_iota(jnp.int32, (span, group * BLOCK), 1) & (BLOCK - 1)
    band = (kj >= qi) & (kj <= qi + 2 * WINDOW)
    units = [(g, t) for g in range(A_KV) for t in range(qblocks)]

    scores = []
    for g, t in units:
        kspan = jnp.concatenate([kl_ref[g], kc_ref[g], kr_ref[g]], axis=0)
        qt = jnp.concatenate(
            [qt_ref[(g * group + h) * HEAD_DIM:(g * group + h + 1) * HEAD_DIM, t * BLOCK:(t + 1) * BLOCK]
             for h in range(group)], axis=1)
        scores.append(_dot(kspan[t * BLOCK:t * BLOCK + span], qt))
    probs = []
    for (g, t), st in zip(units, scores):
        valid = band
        if t == 0:
            valid = valid & (kj >= jnp.where(step == 0, BLOCK, 0))
        if t == qblocks - 1:
            valid = valid & (kj < jnp.where(step == last, 2 * BLOCK, span))
        st = jnp.where(valid, st, NEG_BIG)
        m = jnp.maximum(jnp.max(st, axis=0, keepdims=True), sink_ref[g])
        probs.append((jnp.exp(st - m).astype(BF16), jnp.exp(sink_ref[g] - m)))
    for (g, t), (pt, sink_p) in zip(units, probs):
        rows = slice(g * HEAD_DIM, (g + 1) * HEAD_DIM)
        vts = jnp.concatenate([vtl_ref[rows, :], vtc_ref[rows, :], vtr_ref[rows, :]], axis=1)
        vte = jnp.concatenate([vts, ones_ref[...]], axis=0)
        oe = _dot(vte[:, t * BLOCK:t * BLOCK + span], pt)
        o = oe[0:HEAD_DIM] / (oe[HEAD_DIM:HEAD_DIM + 1] + sink_p)
        stacked = jnp.concatenate([o[:, h * BLOCK:(h + 1) * BLOCK] for h in range(group)], axis=0)
        o_ref[t * BLOCK:(t + 1) * BLOCK, g * group * HEAD_DIM:(g + 1) * group * HEAD_DIM] = (
            stacked.T.astype(BF16))


def _attn_a(sink_rows, qt, k, vt, batch, seq):
    n = batch * seq
    qb = ATTN_A_QBLOCKS if seq % (ATTN_A_QBLOCKS * BLOCK) == 0 else 1
    tq = qb * BLOCK
    steps = seq // tq
    nblk = seq // BLOCK
    left = lambda s: jnp.maximum(s * qb - 1, 0)
    right = lambda s: jnp.minimum((s + 1) * qb, nblk - 1)
    ones = (lax.broadcasted_iota(jnp.int32, (BF16_SUBLANES, tq + 2 * BLOCK), 0) == 0).astype(BF16)
    in_specs = [
        _resident(sink_rows.shape, lambda b, s: (0, 0, 0)),
        pl.BlockSpec((None, A_Q_W, tq), lambda b, s: (b, 0, s)),
        pl.BlockSpec((A_KV, BLOCK, HEAD_DIM), lambda b, s: (0, b * nblk + left(s), 0)),
        pl.BlockSpec((A_KV, tq, HEAD_DIM), lambda b, s: (0, b * steps + s, 0)),
        pl.BlockSpec((A_KV, BLOCK, HEAD_DIM), lambda b, s: (0, b * nblk + right(s), 0)),
        pl.BlockSpec((None, A_KV_W, BLOCK), lambda b, s: (b, 0, left(s))),
        pl.BlockSpec((None, A_KV_W, tq), lambda b, s: (b, 0, s)),
        pl.BlockSpec((None, A_KV_W, BLOCK), lambda b, s: (b, 0, right(s))),
        _resident(ones.shape, lambda b, s: (0, 0)),
    ]
    return pl.pallas_call(
        _attn_a_kernel, grid=(batch, steps), in_specs=in_specs,
        out_specs=pl.BlockSpec((tq, A_Q_W), lambda b, s: (b * steps + s, 0)),
        out_shape=jax.ShapeDtypeStruct((n, A_Q_W), BF16),
        compiler_params=_params(("parallel", "parallel")), name="attn_window",
    )(sink_rows, qt, k, k, k, vt, vt, vt, ones)


def _attn_b_kernel(qt_ref, k_ref, vt_ref, ones_ref, o_ref):
    group = B_HEADS // B_KV
    ones = ones_ref[...]

    def scores(head):
        return _dot(k_ref[head // group], qt_ref[head * HEAD_DIM:(head + 1) * HEAD_DIM, :])

    outs = []
    st_next = scores(0)
    for head in range(B_HEADS):
        st = st_next
        if head + 1 < B_HEADS:
            st_next = scores(head + 1)
        g = head // group
        vte = jnp.concatenate([vt_ref[g * HEAD_DIM:(g + 1) * HEAD_DIM, :], ones], axis=0)
        m = jnp.max(st, axis=0, keepdims=True)
        pt = jnp.exp(st - m).astype(BF16)
        oe = _dot(vte, pt)
        outs.append(oe[0:HEAD_DIM] / oe[HEAD_DIM:HEAD_DIM + 1])
    o_ref[...] = jnp.concatenate(outs, axis=0).T.astype(BF16)


def _attn_b(qt, k, vt, batch, seq):
    n = batch * seq
    tq = min(ATTN_B_TQ, seq)
    steps = seq // tq
    ones = (lax.broadcasted_iota(jnp.int32, (BF16_SUBLANES, seq), 0) == 0).astype(BF16)
    in_specs = [
        pl.BlockSpec((None, B_Q_W, tq), lambda b, s: (b, 0, s)),
        pl.BlockSpec((B_KV, seq, HEAD_DIM), lambda b, s: (0, b, 0)),
        pl.BlockSpec((None, B_KV_W, seq), lambda b, s: (b, 0, 0)),
        _resident((BF16_SUBLANES, seq), lambda b, s: (0, 0)),
    ]
    return pl.pallas_call(
        _attn_b_kernel, grid=(batch, steps), in_specs=in_specs,
        out_specs=pl.BlockSpec((tq, B_Q_W), lambda b, s: (b * steps + s, 0)),
        out_shape=jax.ShapeDtypeStruct((n, B_Q_W), BF16),
        compiler_params=_params(("parallel", "parallel")), name="attn_dense",
    )(qt, k, vt, ones)


def _merge_kernel(alpha, x_ref, a_ref, b_ref, c_ref, wg_ref, bg_ref, wb_ref, wo_ref, g_ref, beta_ref, o_ref):
    x = x_ref[...]
    xb = x.astype(BF16)
    merged = None
    for i, br in enumerate((a_ref, b_ref, c_ref)):
        cols = slice(i * D_MODEL, (i + 1) * D_MODEL)
        gate = jax.nn.sigmoid(_dot(xb, wg_ref[:, cols]) + bg_ref[:, cols])
        t = _dot(br[...], wb_ref[i]) * gate
        merged = t if merged is None else merged + t
    h = _dot(merged.astype(BF16), wo_ref[...])
    o_ref[...] = _layer_norm(alpha * x + h, g_ref[...], beta_ref[...])


def _merge(alpha, x, oa, ob, oc, wg, bg, wb, wo, lg, lb, layer, tm):
    n = x.shape[0]
    row = lambda i: (i, 0)
    lay3 = lambda i: (layer, 0, 0)
    lay4 = lambda i: (layer, 0, 0, 0)
    in_specs = [
        pl.BlockSpec((tm, D_MODEL), row),
        pl.BlockSpec((tm, A_Q_W), row), pl.BlockSpec((tm, B_Q_W), row), pl.BlockSpec((tm, C_WIDTH), row),
        _resident((None, D_MODEL, GATE_W), lay3),
        _resident((None, 1, GATE_W), lay3),
        _resident((None, N_BRANCH, C_WIDTH, D_MODEL), lay4),
        _resident((None, D_MODEL, D_MODEL), lay3),
        _resident((None, 1, D_MODEL), lay3), _resident((None, 1, D_MODEL), lay3),
    ]
    return pl.pallas_call(
        functools.partial(_merge_kernel, alpha), grid=(n // tm,), in_specs=in_specs,
        out_specs=pl.BlockSpec((tm, D_MODEL), row),
        out_shape=jax.ShapeDtypeStruct((n, D_MODEL), F32),
        compiler_params=_params(("parallel",)), name="merge",
    )(x, oa, ob, oc, wg, bg, wb, wo, lg, lb)


def _memkv_kernel(m_ref, w_ref, kt_ref, v_ref):
    kv = _dot(m_ref[...].astype(BF16), w_ref[...])
    kt_ref[...] = kv[:, 0:X_WIDTH].T.astype(BF16)
    v_ref[...] = kv[:, X_WIDTH:2 * X_WIDTH].astype(BF16)


def _memkv(mem, wkv):
    batch, mlen, _ = mem.shape
    depth = wkv.shape[0]
    return pl.pallas_call(
        _memkv_kernel, grid=(depth, batch),
        in_specs=[pl.BlockSpec((None, mlen, D_MODEL), lambda l, b: (b, 0, 0)),
                  pl.BlockSpec((None, D_MODEL, 2 * X_WIDTH), lambda l, b: (l, 0, 0))],
        out_specs=(pl.BlockSpec((None, None, X_WIDTH, mlen), lambda l, b: (l, b, 0, 0)),
                   pl.BlockSpec((None, None, mlen, X_WIDTH), lambda l, b: (l, b, 0, 0))),
        out_shape=(jax.ShapeDtypeStruct((depth, batch, X_WIDTH, mlen), BF16),
                   jax.ShapeDtypeStruct((depth, batch, mlen, X_WIDTH), BF16)),
        compiler_params=_params(("parallel", "parallel")), name="mem_kv",
    )(mem, wkv)


def _xattn_kernel(alpha, x_ref, wq_ref, kt_ref, v_ref, wo_ref, g_ref, beta_ref, o_ref):
    x = x_ref[...]
    q = (_dot(x.astype(BF16), wq_ref[...]) * (1.0 / math.sqrt(X_HEAD_DIM))).astype(BF16)
    heads = [slice(h * X_HEAD_DIM, (h + 1) * X_HEAD_DIM) for h in range(X_HEADS)]
    scores = [_dot(q[:, cols], kt_ref[cols, :]) for cols in heads]
    probs = []
    for s in scores:
        p = jnp.exp(s - jnp.max(s, axis=-1, keepdims=True))
        probs.append((p.astype(BF16), jnp.sum(p, axis=-1, keepdims=True)))
    outs = [_dot(p, v_ref[:, cols]) / denom for cols, (p, denom) in zip(heads, probs)]
    att = jnp.concatenate(outs, axis=1).astype(BF16)
    h_out = _dot(att, wo_ref[...])
    o_ref[...] = _layer_norm(alpha * x + h_out, g_ref[...], beta_ref[...])


def _xattn(alpha, x, wq, kt, v, wo, lg, lb, layer, batch, seq, tm):
    n = x.shape[0]
    spb = seq // tm
    mlen = kt.shape[-1]
    lay3 = lambda b, s: (layer, 0, 0)
    in_specs = [
        pl.BlockSpec((tm, D_MODEL), lambda b, s: (b * spb + s, 0)),
        _resident((None, D_MODEL, X_WIDTH), lay3),
        pl.BlockSpec((None, None, X_WIDTH, mlen), lambda b, s: (layer, b, 0, 0)),
        pl.BlockSpec((None, None, mlen, X_WIDTH), lambda b, s: (layer, b, 0, 0)),
        _resident((None, X_WIDTH, D_MODEL), lay3),
        _resident((None, 1, D_MODEL), lay3), _resident((None, 1, D_MODEL), lay3),
    ]
    return pl.pallas_call(
        functools.partial(_xattn_kernel, alpha), grid=(batch, spb), in_specs=in_specs,
        out_specs=pl.BlockSpec((tm, D_MODEL), lambda b, s: (b * spb + s, 0)),
        out_shape=jax.ShapeDtypeStruct((n, D_MODEL), F32),
        compiler_params=_params(("parallel", "parallel")), name="cross_attn",
    )(x, wq, kt, v, wo, lg, lb)


def _ffn_kernel(alpha, xp_ref, x_ref, xn_ref, wu_ref, ck_ref, cb_ref, wd_ref, g_ref, beta_ref, o_ref,
                xe_ref, h_ref):
    tm = x_ref.shape[0]
    step = pl.program_id(1)
    last = pl.num_programs(1) - 1
    x = x_ref[...]
    xe_ref[0:HALO, :] = jnp.where(step == 0, 0.0, xp_ref[...]).astype(BF16)
    xe_ref[HALO:HALO + tm, :] = x.astype(BF16)
    xe_ref[HALO + tm:2 * HALO + tm, :] = jnp.where(step == last, 0.0, xn_ref[...]).astype(BF16)
    nchunk = wd_ref.shape[0]

    def halves(c):
        return (slice(c * FF_CHUNK, (c + 1) * FF_CHUNK), slice(D_FF + c * FF_CHUNK, D_FF + (c + 1) * FF_CHUNK))

    def up(c):
        for i, cols in enumerate(halves(c)):
            h_ref[c % 2, :, i * FF_CHUNK:(i + 1) * FF_CHUNK] = _dot(xe_ref[...], wu_ref[:, cols])

    def gate(c):
        hs = h_ref.at[c % 2]
        conv = []
        for i, cols in enumerate(halves(c)):
            ck = ck_ref[:, cols]
            lanes = slice(i * FF_CHUNK, (i + 1) * FF_CHUNK)
            conv.append(hs[HALO - 1:HALO - 1 + tm, lanes] * ck[0:1]
                        + hs[HALO:HALO + tm, lanes] * ck[1:2]
                        + hs[HALO + 1:HALO + 1 + tm, lanes] * ck[2:3]
                        + cb_ref[:, cols])
        return (_gelu(conv[0]) * conv[1]).astype(BF16)

    up(0)
    acc = None
    for c in range(nchunk):
        if c + 1 < nchunk:
            up(c + 1)
        d = _dot(gate(c), wd_ref[c])
        acc = d if acc is None else acc + d
    o_ref[...] = _layer_norm(alpha * x + acc, g_ref[...], beta_ref[...])


def _ffn(alpha, x, wu, ck, cb, wd, lg, lb, layer, batch, seq, tm):
    n = x.shape[0]
    spb = seq // tm
    hpt = tm // HALO
    nhalo = n // HALO
    nchunk = wd.shape[1]
    lay3 = lambda b, s: (layer, 0, 0)
    lay4 = lambda b, s: (layer, 0, 0, 0)
    prev = lambda b, s: (jnp.maximum((b * spb + s) * hpt - 1, 0), 0)
    nxt = lambda b, s: (jnp.minimum((b * spb + s + 1) * hpt, nhalo - 1), 0)
    in_specs = [
        pl.BlockSpec((HALO, D_MODEL), prev),
        pl.BlockSpec((tm, D_MODEL), lambda b, s: (b * spb + s, 0)),
        pl.BlockSpec((HALO, D_MODEL), nxt),
        _resident((None, D_MODEL, 2 * D_FF), lay3),
        _resident((None, CONV_W, 2 * D_FF), lay3),
        _resident((None, 1, 2 * D_FF), lay3),
        _resident((None, nchunk, FF_CHUNK, D_MODEL), lay4),
        _resident((None, 1, D_MODEL), lay3), _resident((None, 1, D_MODEL), lay3),
    ]
    return pl.pallas_call(
        functools.partial(_ffn_kernel, alpha), grid=(batch, spb), in_specs=in_specs,
        out_specs=pl.BlockSpec((tm, D_MODEL), lambda b, s: (b * spb + s, 0)),
        out_shape=jax.ShapeDtypeStruct((n, D_MODEL), F32),
        scratch_shapes=[pltpu.VMEM((tm + 2 * HALO, D_MODEL), BF16),
                        pltpu.VMEM((2, tm + 2 * HALO, 2 * FF_CHUNK), F32)],
        compiler_params=_params(("parallel", "parallel")), name="conv_ffn",
    )(x, x, x, wu, ck, cb, wd, lg, lb)


def _rope_tables(seq):
    def table(pos_parts, half):
        inv = ROPE_THETA ** (-jnp.arange(half, dtype=F32) * (1.0 / half))
        cos, sin = [], []
        for pos in pos_parts:
            ang = pos.astype(F32)[:, None] * inv[None, :]
            cos += [jnp.cos(ang), jnp.cos(ang)]
            sin += [-jnp.sin(ang), jnp.sin(ang)]
        cos = jnp.concatenate(cos, axis=1)
        sin = jnp.concatenate(sin, axis=1)
        reps = LANES // cos.shape[1]
        return jnp.tile(cos, (1, reps)), jnp.tile(sin, (1, reps))

    pos = jnp.arange(seq, dtype=jnp.int32)
    cosa, sina = table([pos], HEAD_DIM // 2)
    cosb, sinb = table([pos // GRID_W, pos % GRID_W], HEAD_DIM // 4)
    return cosa, sina, cosb, sinb


def kernel(x, mem, w_in, b_gate, a_sink, b_q_gain, b_k_gain, c_ln_g, c_ln_b, c_ws, c_bs, w_branch, w_mix_out,
           ln1_g, ln1_b, x_wq, x_wkv, x_wo, ln2_g, ln2_b, f_w_up, f_conv_k, f_conv_b, f_w_down, ln3_g, ln3_b):
    batch, seq, _ = x.shape
    depth = w_in.shape[0]
    n = batch * seq
    alpha = (2 * depth) ** 0.25
    tm = min(ROW_TILE, seq)
    tm_ffn = min(FFN_ROW_TILE, seq)
    assert seq % tm == 0 and tm % CHUNK == 0 and seq % GRID_W == 0 and seq % tm_ffn == 0 and tm_ffn % HALO == 0
    assert D_FF % FF_CHUNK == 0
    nchunk = D_FF // FF_CHUNK

    w_main = w_in[:, :, 0:BRANCH_IN_W].astype(BF16)
    w_gate = w_in[:, :, BRANCH_IN_W:].astype(BF16)
    bg = b_gate.reshape(depth, 1, GATE_W)
    qg = jnp.tile(b_q_gain, (1, B_HEADS)).reshape(depth, 1, B_Q_W)
    kg = jnp.tile(b_k_gain, (1, B_KV)).reshape(depth, 1, B_KV_W)
    head_of = jnp.arange(B_Q_W, dtype=jnp.int32) // HEAD_DIM
    bd = (head_of[:, None] == head_of[None, :]).astype(BF16)
    lng = c_ln_g.reshape(depth, 1, C_WIDTH)
    lnb = c_ln_b.reshape(depth, 1, C_WIDTH)
    ws = c_ws.astype(BF16)
    bsb = jnp.broadcast_to(c_bs[:, :, :, None], (depth, C_GROUPS, CHUNK, C_WIDTH // C_GROUPS))
    wb = w_branch.astype(BF16)
    wo = w_mix_out.astype(BF16)
    wq = x_wq.astype(BF16)
    wkv = x_wkv.astype(BF16)
    xwo = x_wo.astype(BF16)

    wu = f_w_up.astype(BF16)
    ck = f_conv_k
    cb = f_conv_b.reshape(depth, 1, 2 * D_FF)
    wd = f_w_down.astype(BF16).reshape(depth, nchunk, FF_CHUNK, D_MODEL)
    vec = lambda a: a.reshape(depth, 1, D_MODEL)
    l1g, l1b, l2g, l2b, l3g, l3b = map(vec, (ln1_g, ln1_b, ln2_g, ln2_b, ln3_g, ln3_b))
    tabs = _rope_tables(seq)
    sink_rows = jnp.repeat(a_sink, BLOCK, axis=1).reshape(depth, A_KV, 1, (A_HEADS // A_KV) * BLOCK)

    mkt, mv = _memkv(mem, wkv)
    xs = x.reshape(n, D_MODEL)
    for l in range(depth):
        aqt, ak, avt, bqt, bk, bvt, oc = _inproj(xs, w_main, tabs, qg, kg, bd, lng, lnb, ws, bsb, l, batch, seq, tm)
        oa = _attn_a(sink_rows[l], aqt, ak, avt, batch, seq)
        ob = _attn_b(bqt, bk, bvt, batch, seq)
        xs = _merge(alpha, xs, oa, ob, oc, w_gate, bg, wb, wo, l1g, l1b, l, tm)
        xs = _xattn(alpha, xs, wq, mkt, mv, xwo, l2g, l2b, l, batch, seq, tm)
        xs = _ffn(alpha, xs, wu, ck, cb, wd, l3g, l3b, l, batch, seq, tm_ffn)
    return xs.reshape(batch, seq, D_MODEL)
```

```python
import functools
import math

import jax
import jax.numpy as jnp
from jax import lax
from jax.experimental import pallas as pl
from jax.experimental.pallas import tpu as pltpu

F32 = jnp.float32
BF16 = jnp.bfloat16

D_MODEL = 1024
HEAD_DIM = 64
BLOCK = 128
A_HEADS = 8
A_KV = 2
WINDOW = 128
B_HEADS = 8
B_KV = 2
GRID_W = 64
C_WIDTH = 512
C_GROUPS = 4
CHUNK = 128
N_BRANCH = 3
ROPE_THETA = 10000.0
X_HEADS = 4
X_HEAD_DIM = 128
X_WIDTH = X_HEADS * X_HEAD_DIM
D_FF = 2816
CONV_W = 3
LN_EPS = 1e-5
RMS_EPS = 1e-6

A_Q_W = A_HEADS * HEAD_DIM
A_KV_W = A_KV * HEAD_DIM
B_Q_W = B_HEADS * HEAD_DIM
B_KV_W = B_KV * HEAD_DIM
QKV_W = A_Q_W + 2 * A_KV_W
BRANCH_IN_W = 2 * QKV_W + 2 * C_WIDTH
GATE_W = N_BRANCH * D_MODEL

LANES = 128
ROW_TILE = 1024
FFN_ROW_TILE = 512
ATTN_B_TQ = 512
ATTN_A_QBLOCKS = 4
FF_CHUNK = 256
FF_DOWN_GROUP = 11
BF16_SUBLANES = 16
HALO = BF16_SUBLANES
VMEM_LIMIT = 56 * 1024 * 1024
NEG_BIG = -1e30
LOG2E = 1.0 / math.log(2.0)


def _dot(a, b):
    return jnp.dot(a, b, preferred_element_type=F32)


def _layer_norm(y, g, b):
    mu = jnp.mean(y, axis=-1, keepdims=True)
    d = y - mu
    var = jnp.mean(d * d, axis=-1, keepdims=True)
    return d * lax.rsqrt(var + LN_EPS) * g + b


def _gelu(x):
    return 0.5 * x * (1.0 + lax.erf(x * (1.0 / math.sqrt(2.0))))


def _resident(shape, index_map):
    return pl.BlockSpec(shape, index_map, pipeline_mode=pl.Buffered(1))


def _params(sem):
    return pltpu.CompilerParams(dimension_semantics=sem, vmem_limit_bytes=VMEM_LIMIT)


def _inproj_kernel(x_ref, w_ref, cosa_ref, sina_ref, cosb_ref, sinb_ref, qg_ref, kg_ref, bd_ref,
                   lng_ref, lnb_ref, ws_ref, bsb_ref,
                   aqt_ref, ak_ref, avt_ref, bqt_ref, bk_ref, bvt_ref, oc_ref):
    tm = x_ref.shape[0]
    xb = x_ref[...].astype(BF16)
    lane = lax.broadcasted_iota(jnp.int32, (tm, LANES), 1)

    pa = _dot(xb, w_ref[:, 0:QKV_W])
    cosa = cosa_ref[...]
    sina = sina_ref[...]
    first_a = (lane & (HEAD_DIM - 1)) < (HEAD_DIM // 2)

    def rope_a(seg):
        sw = jnp.where(first_a, pltpu.roll(seg, LANES - HEAD_DIM // 2, 1), pltpu.roll(seg, HEAD_DIM // 2, 1))
        return seg * cosa + sw * sina

    scale = 1.0 / math.sqrt(HEAD_DIM)
    for j in range(A_Q_W // LANES):
        sl = slice(j * LANES, (j + 1) * LANES)
        aqt_ref[sl, :] = (rope_a(pa[:, sl]) * (scale * LOG2E)).T.astype(BF16)
    akr = rope_a(pa[:, A_Q_W:A_Q_W + A_KV_W]).astype(BF16)
    for g in range(A_KV):
        ak_ref[g] = akr[:, g * HEAD_DIM:(g + 1) * HEAD_DIM]
    avt_ref[...] = pa[:, A_Q_W + A_KV_W:QKV_W].T.astype(BF16)

    pb = _dot(xb, w_ref[:, QKV_W:2 * QKV_W])
    cosb = cosb_ref[...]
    sinb = sinb_ref[...]
    quarter = HEAD_DIM // 4
    first_b = (lane & (2 * quarter - 1)) < quarter

    def rope_b(seg):
        sw = jnp.where(first_b, pltpu.roll(seg, LANES - quarter, 1), pltpu.roll(seg, quarter, 1))
        return seg * cosb + sw * sinb

    def head_sumsq(v, bd):
        sq = v * v
        hi = sq.astype(BF16)
        lo = (sq - hi.astype(F32)).astype(BF16)
        return _dot(hi, bd) + _dot(lo, bd)

    bq = pb[:, 0:B_Q_W]
    bqn = bq * lax.rsqrt(head_sumsq(bq, bd_ref[...]) * (1.0 / HEAD_DIM) + RMS_EPS) * qg_ref[...]
    for j in range(B_Q_W // LANES):
        sl = slice(j * LANES, (j + 1) * LANES)
        bqt_ref[sl, :] = (rope_b(bqn[:, sl]) * (scale * LOG2E)).T.astype(BF16)
    bk = pb[:, B_Q_W:B_Q_W + B_KV_W]
    bkn = bk * lax.rsqrt(head_sumsq(bk, bd_ref[0:B_KV_W, 0:B_KV_W]) * (1.0 / HEAD_DIM) + RMS_EPS) * kg_ref[...]
    bkr = rope_b(bkn).astype(BF16)
    for g in range(B_KV):
        bk_ref[g] = bkr[:, g * HEAD_DIM:(g + 1) * HEAD_DIM]
    bvt_ref[...] = pb[:, B_Q_W + B_KV_W:QKV_W].T.astype(BF16)

    pc = _gelu(_dot(xb, w_ref[:, 2 * QKV_W:BRANCH_IN_W]))
    u = pc[:, 0:C_WIDTH]
    vln = _layer_norm(pc[:, C_WIDTH:2 * C_WIDTH], lng_ref[...], lnb_ref[...]).astype(BF16)
    nch = tm // CHUNK
    gw = C_WIDTH // C_GROUPS
    for g in range(C_GROUPS):
        cs = slice(g * gw, (g + 1) * gw)
        rhs = jnp.concatenate([vln[r * CHUNK:(r + 1) * CHUNK, cs] for r in range(nch)], axis=1)
        mixed = _dot(ws_ref[g], rhs)
        for r in range(nch):
            rs = slice(r * CHUNK, (r + 1) * CHUNK)
            oc_ref[rs, cs] = (u[rs, cs] * (mixed[:, r * gw:(r + 1) * gw] + bsb_ref[g])).astype(BF16)


def _inproj(x, w, tabs, qg, kg, bd, lng, lnb, ws, bsb, layer, batch, seq, tm):
    n = x.shape[0]
    spb = seq // tm
    row = lambda i: (i, 0)
    tab = lambda i: (i % spb, 0)
    lay3 = lambda i: (layer, 0, 0)
    lay4 = lambda i: (layer, 0, 0, 0)
    tr = lambda i: (i // spb, 0, i % spb)
    out_shape = (
        jax.ShapeDtypeStruct((batch, A_Q_W, seq), BF16),
        jax.ShapeDtypeStruct((A_KV, n, HEAD_DIM), BF16),
        jax.ShapeDtypeStruct((batch, A_KV_W, seq), BF16),
        jax.ShapeDtypeStruct((batch, B_Q_W, seq), BF16),
        jax.ShapeDtypeStruct((B_KV, n, HEAD_DIM), BF16),
        jax.ShapeDtypeStruct((batch, B_KV_W, seq), BF16),
        jax.ShapeDtypeStruct((n, C_WIDTH), BF16),
    )
    in_specs = [
        pl.BlockSpec((tm, D_MODEL), row),
        _resident((None, D_MODEL, BRANCH_IN_W), lay3),
        pl.BlockSpec((tm, LANES), tab), pl.BlockSpec((tm, LANES), tab),
        pl.BlockSpec((tm, LANES), tab), pl.BlockSpec((tm, LANES), tab),
        _resident((None, 1, B_Q_W), lay3), _resident((None, 1, B_KV_W), lay3),
        _resident((B_Q_W, B_Q_W), lambda i: (0, 0)),
        _resident((None, 1, C_WIDTH), lay3), _resident((None, 1, C_WIDTH), lay3),
        _resident((None, C_GROUPS, CHUNK, CHUNK), lay4),
        _resident((None, C_GROUPS, CHUNK, C_WIDTH // C_GROUPS), lay4),
    ]
    out_specs = (
        pl.BlockSpec((None, A_Q_W, tm), tr),
        pl.BlockSpec((A_KV, tm, HEAD_DIM), lambda i: (0, i, 0)),
        pl.BlockSpec((None, A_KV_W, tm), tr),
        pl.BlockSpec((None, B_Q_W, tm), tr),
        pl.BlockSpec((B_KV, tm, HEAD_DIM), lambda i: (0, i, 0)),
        pl.BlockSpec((None, B_KV_W, tm), tr),
        pl.BlockSpec((tm, C_WIDTH), row),
    )
    return pl.pallas_call(
        _inproj_kernel, grid=(n // tm,), in_specs=in_specs, out_specs=out_specs, out_shape=out_shape,
        compiler_params=_params(("parallel",)), name="inproj",
    )(x, w, *tabs, qg, kg, bd, lng, lnb, ws, bsb)


def _attn_a_kernel(sink_ref, qt_ref, kl_ref, kc_ref, kr_ref, vtl_ref, vtc_ref, vtr_ref, ones_ref, o_ref):
    qblocks = qt_ref.shape[1] // BLOCK
    group = A_HEADS // A_KV
    step = pl.program_id(1)
    last = pl.num_programs(1) - 1
    span = 3 * BLOCK
    kj = lax.broadcasted_iota(jnp.int32, (span, group * BLOCK), 0)
    qi = lax.broadcasted_iota(jnp.int32, (span, group * BLOCK), 1) & (BLOCK - 1)
    band = (kj >= qi) & (kj <= qi + 2 * WINDOW)
    units = [(g, t) for g in range(A_KV) for t in range(qblocks)]

    scores = []
    for g, t in units:
        kspan = jnp.concatenate([kl_ref[g], kc_ref[g], kr_ref[g]], axis=0)
        qt = jnp.concatenate(
            [qt_ref[(g * group + h) * HEAD_DIM:(g * group + h + 1) * HEAD_DIM, t * BLOCK:(t + 1) * BLOCK]
             for h in range(group)], axis=1)
        scores.append(_dot(kspan[t * BLOCK:t * BLOCK + span], qt))
    probs = []
    for (g, t), st in zip(units, scores):
        valid = band
        if t == 0:
            valid = valid & (kj >= jnp.where(step == 0, BLOCK, 0))
        if t == qblocks - 1:
            valid = valid & (kj < jnp.where(step == last, 2 * BLOCK, span))
        st = jnp.where(valid, st, NEG_BIG)
        m = jnp.maximum(jnp.max(st, axis=0, keepdims=True), sink_ref[g])
        probs.append((jnp.exp2(st - m).astype(BF16), jnp.exp2(sink_ref[g] - m)))
    for (g, t), (pt, sink_p) in zip(units, probs):
        rows = slice(g * HEAD_DIM, (g + 1) * HEAD_DIM)
        vts = jnp.concatenate([vtl_ref[rows, :], vtc_ref[rows, :], vtr_ref[rows, :]], axis=1)
        vte = jnp.concatenate([vts, ones_ref[...]], axis=0)
        oe = _dot(vte[:, t * BLOCK:t * BLOCK + span], pt)
        o = oe[0:HEAD_DIM] / (oe[HEAD_DIM:HEAD_DIM + 1] + sink_p)
        stacked = jnp.concatenate([o[:, h * BLOCK:(h + 1) * BLOCK] for h in range(group)], axis=0)
        o_ref[t * BLOCK:(t + 1) * BLOCK, g * group * HEAD_DIM:(g + 1) * group * HEAD_DIM] = (
            stacked.T.astype(BF16))


def _attn_a(sink_rows, qt, k, vt, batch, seq):
    n = batch * seq
    qb = ATTN_A_QBLOCKS if seq % (ATTN_A_QBLOCKS * BLOCK) == 0 else 1
    tq = qb * BLOCK
    steps = seq // tq
    nblk = seq // BLOCK
    left = lambda s: jnp.maximum(s * qb - 1, 0)
    right = lambda s: jnp.minimum((s + 1) * qb, nblk - 1)
    ones = (lax.broadcasted_iota(jnp.int32, (BF16_SUBLANES, tq + 2 * BLOCK), 0) == 0).astype(BF16)
    in_specs = [
        _resident(sink_rows.shape, lambda b, s: (0, 0, 0)),
        pl.BlockSpec((None, A_Q_W, tq), lambda b, s: (b, 0, s)),
        pl.BlockSpec((A_KV, BLOCK, HEAD_DIM), lambda b, s: (0, b * nblk + left(s), 0)),
        pl.BlockSpec((A_KV, tq, HEAD_DIM), lambda b, s: (0, b * steps + s, 0)),
        pl.BlockSpec((A_KV, BLOCK, HEAD_DIM), lambda b, s: (0, b * nblk + right(s), 0)),
        pl.BlockSpec((None, A_KV_W, BLOCK), lambda b, s: (b, 0, left(s))),
        pl.BlockSpec((None, A_KV_W, tq), lambda b, s: (b, 0, s)),
        pl.BlockSpec((None, A_KV_W, BLOCK), lambda b, s: (b, 0, right(s))),
        _resident(ones.shape, lambda b, s: (0, 0)),
    ]
    return pl.pallas_call(
        _attn_a_kernel, grid=(batch, steps), in_specs=in_specs,
        out_specs=pl.BlockSpec((tq, A_Q_W), lambda b, s: (b * steps + s, 0)),
        out_shape=jax.ShapeDtypeStruct((n, A_Q_W), BF16),
        compiler_params=_params(("parallel", "parallel")), name="attn_window",
    )(sink_rows, qt, k, k, k, vt, vt, vt, ones)


def _attn_b_kernel(qt_ref, k_ref, vt_ref, ones_ref, o_ref):
    group = B_HEADS // B_KV
    ones = ones_ref[...]

    def scores(head):
        return _dot(k_ref[head // group], qt_ref[head * HEAD_DIM:(head + 1) * HEAD_DIM, :])

    outs = []
    st_next = scores(0)
    for head in range(B_HEADS):
        st = st_next
        if head + 1 < B_HEADS:
            st_next = scores(head + 1)
        g = head // group
        vte = jnp.concatenate([vt_ref[g * HEAD_DIM:(g + 1) * HEAD_DIM, :], ones], axis=0)
        m = jnp.max(st, axis=0, keepdims=True)
        pt = jnp.exp2(st - m).astype(BF16)
        oe = _dot(vte, pt)
        outs.append(oe[0:HEAD_DIM] / oe[HEAD_DIM:HEAD_DIM + 1])
    o_ref[...] = jnp.concatenate(outs, axis=0).T.astype(BF16)


def _attn_b(qt, k, vt, batch, seq):
    n = batch * seq
    tq = min(ATTN_B_TQ, seq)
    steps = seq // tq
    ones = (lax.broadcasted_iota(jnp.int32, (BF16_SUBLANES, seq), 0) == 0).astype(BF16)
    in_specs = [
        pl.BlockSpec((None, B_Q_W, tq), lambda b, s: (b, 0, s)),
        pl.BlockSpec((B_KV, seq, HEAD_DIM), lambda b, s: (0, b, 0)),
        pl.BlockSpec((None, B_KV_W, seq), lambda b, s: (b, 0, 0)),
        _resident((BF16_SUBLANES, seq), lambda b, s: (0, 0)),
    ]
    return pl.pallas_call(
        _attn_b_kernel, grid=(batch, steps), in_specs=in_specs,
        out_specs=pl.BlockSpec((tq, B_Q_W), lambda b, s: (b * steps + s, 0)),
        out_shape=jax.ShapeDtypeStruct((n, B_Q_W), BF16),
        compiler_params=_params(("parallel", "parallel")), name="attn_dense",
    )(qt, k, vt, ones)


def _merge_kernel(alpha, x_ref, a_ref, b_ref, c_ref, wg_ref, bg_ref, wb_ref, wo_ref, g_ref, beta_ref, o_ref):
    x = x_ref[...]
    xb = x.astype(BF16)
    merged = None
    for i, br in enumerate((a_ref, b_ref, c_ref)):
        cols = slice(i * D_MODEL, (i + 1) * D_MODEL)
        gate = jax.nn.sigmoid(_dot(xb, wg_ref[:, cols]) + bg_ref[:, cols])
        t = _dot(br[...], wb_ref[i]) * gate
        merged = t if merged is None else merged + t
    h = _dot(merged.astype(BF16), wo_ref[...])
    o_ref[...] = _layer_norm(alpha * x + h, g_ref[...], beta_ref[...])


def _merge(alpha, x, oa, ob, oc, wg, bg, wb, wo, lg, lb, layer, tm):
    n = x.shape[0]
    row = lambda i: (i, 0)
    lay3 = lambda i: (layer, 0, 0)
    lay4 = lambda i: (layer, 0, 0, 0)
    in_specs = [
        pl.BlockSpec((tm, D_MODEL), row),
        pl.BlockSpec((tm, A_Q_W), row), pl.BlockSpec((tm, B_Q_W), row), pl.BlockSpec((tm, C_WIDTH), row),
        _resident((None, D_MODEL, GATE_W), lay3),
        _resident((None, 1, GATE_W), lay3),
        _resident((None, N_BRANCH, C_WIDTH, D_MODEL), lay4),
        _resident((None, D_MODEL, D_MODEL), lay3),
        _resident((None, 1, D_MODEL), lay3), _resident((None, 1, D_MODEL), lay3),
    ]
    return pl.pallas_call(
        functools.partial(_merge_kernel, alpha), grid=(n // tm,), in_specs=in_specs,
        out_specs=pl.BlockSpec((tm, D_MODEL), row),
        out_shape=jax.ShapeDtypeStruct((n, D_MODEL), F32),
        compiler_params=_params(("parallel",)), name="merge",
    )(x, oa, ob, oc, wg, bg, wb, wo, lg, lb)


def _memkv_kernel(m_ref, w_ref, kt_ref, v_ref):
    kv = _dot(m_ref[...].astype(BF16), w_ref[...])
    kt_ref[...] = kv[:, 0:X_WIDTH].T.astype(BF16)
    v_ref[...] = kv[:, X_WIDTH:2 * X_WIDTH].astype(BF16)


def _memkv(mem, wkv):
    batch, mlen, _ = mem.shape
    depth = wkv.shape[0]
    return pl.pallas_call(
        _memkv_kernel, grid=(depth, batch),
        in_specs=[pl.BlockSpec((None, mlen, D_MODEL), lambda l, b: (b, 0, 0)),
                  pl.BlockSpec((None, D_MODEL, 2 * X_WIDTH), lambda l, b: (l, 0, 0))],
        out_specs=(pl.BlockSpec((None, None, X_WIDTH, mlen), lambda l, b: (l, b, 0, 0)),
                   pl.BlockSpec((None, None, mlen, X_WIDTH), lambda l, b: (l, b, 0, 0))),
        out_shape=(jax.ShapeDtypeStruct((depth, batch, X_WIDTH, mlen), BF16),
                   jax.ShapeDtypeStruct((depth, batch, mlen, X_WIDTH), BF16)),
        compiler_params=_params(("parallel", "parallel")), name="mem_kv",
    )(mem, wkv)


def _xattn_kernel(alpha, x_ref, wq_ref, kt_ref, v_ref, wo_ref, g_ref, beta_ref, o_ref):
    x = x_ref[...]
    q = (_dot(x.astype(BF16), wq_ref[...]) * (LOG2E / math.sqrt(X_HEAD_DIM))).astype(BF16)
    heads = [slice(h * X_HEAD_DIM, (h + 1) * X_HEAD_DIM) for h in range(X_HEADS)]
    scores = [_dot(q[:, cols], kt_ref[cols, :]) for cols in heads]
    probs = []
    for s in scores:
        p = jnp.exp2(s - jnp.max(s, axis=-1, keepdims=True))
        probs.append((p.astype(BF16), jnp.sum(p, axis=-1, keepdims=True)))
    outs = [_dot(p, v_ref[:, cols]) / denom for cols, (p, denom) in zip(heads, probs)]
    att = jnp.concatenate(outs, axis=1).astype(BF16)
    h_out = _dot(att, wo_ref[...])
    o_ref[...] = _layer_norm(alpha * x + h_out, g_ref[...], beta_ref[...])


def _xattn(alpha, x, wq, kt, v, wo, lg, lb, layer, batch, seq, tm):
    n = x.shape[0]
    spb = seq // tm
    mlen = kt.shape[-1]
    lay3 = lambda b, s: (layer, 0, 0)
    in_specs = [
        pl.BlockSpec((tm, D_MODEL), lambda b, s: (b * spb + s, 0)),
        _resident((None, D_MODEL, X_WIDTH), lay3),
        pl.BlockSpec((None, None, X_WIDTH, mlen), lambda b, s: (layer, b, 0, 0)),
        pl.BlockSpec((None, None, mlen, X_WIDTH), lambda b, s: (layer, b, 0, 0)),
        _resident((None, X_WIDTH, D_MODEL), lay3),
        _resident((None, 1, D_MODEL), lay3), _resident((None, 1, D_MODEL), lay3),
    ]
    return pl.pallas_call(
        functools.partial(_xattn_kernel, alpha), grid=(batch, spb), in_specs=in_specs,
        out_specs=pl.BlockSpec((tm, D_MODEL), lambda b, s: (b * spb + s, 0)),
        out_shape=jax.ShapeDtypeStruct((n, D_MODEL), F32),
        compiler_params=_params(("parallel", "parallel")), name="cross_attn",
    )(x, wq, kt, v, wo, lg, lb)


def _ffn_kernel(alpha, xp_ref, x_ref, xn_ref, wu_ref, ck_ref, cb_ref, wd_ref, g_ref, beta_ref, o_ref,
                xe_ref, h_ref):
    tm = x_ref.shape[0]
    step = pl.program_id(1)
    last = pl.num_programs(1) - 1
    x = x_ref[...]
    xe_ref[0:HALO, :] = jnp.where(step == 0, 0.0, xp_ref[...]).astype(BF16)
    xe_ref[HALO:HALO + tm, :] = x.astype(BF16)
    xe_ref[HALO + tm:2 * HALO + tm, :] = jnp.where(step == last, 0.0, xn_ref[...]).astype(BF16)
    nchunk = D_FF // FF_CHUNK

    def halves(c):
        return (slice(c * FF_CHUNK, (c + 1) * FF_CHUNK), slice(D_FF + c * FF_CHUNK, D_FF + (c + 1) * FF_CHUNK))

    def up(c):
        for i, cols in enumerate(halves(c)):
            h_ref[c % 2, :, i * FF_CHUNK:(i + 1) * FF_CHUNK] = _dot(xe_ref[...], wu_ref[:, cols])

    def gate(c):
        hs = h_ref.at[c % 2]
        conv = []
        for i, cols in enumerate(halves(c)):
            ck = ck_ref[:, cols]
            lanes = slice(i * FF_CHUNK, (i + 1) * FF_CHUNK)
            conv.append(hs[HALO - 1:HALO - 1 + tm, lanes] * ck[0:1]
                        + hs[HALO:HALO + tm, lanes] * ck[1:2]
                        + hs[HALO + 1:HALO + 1 + tm, lanes] * ck[2:3]
                        + cb_ref[:, cols])
        a = conv[0]
        return (a * (1.0 + lax.erf(a * (1.0 / math.sqrt(2.0)))) * conv[1]).astype(BF16)

    up(0)
    acc = None
    acts = []
    for c in range(nchunk):
        if c + 1 < nchunk:
            up(c + 1)
        acts.append(gate(c))
        if len(acts) == FF_DOWN_GROUP or c + 1 == nchunk:
            first = c + 1 - len(acts)
            act = acts[0] if len(acts) == 1 else jnp.concatenate(acts, axis=1)
            d = _dot(act, wd_ref[first * FF_CHUNK:(c + 1) * FF_CHUNK, :])
            acc = d if acc is None else acc + d
            acts = []
    o_ref[...] = _layer_norm(alpha * x + acc, g_ref[...], beta_ref[...])


def _ffn(alpha, x, wu, ck, cb, wd, lg, lb, layer, batch, seq, tm):
    n = x.shape[0]
    spb = seq // tm
    hpt = tm // HALO
    nhalo = n // HALO
    lay3 = lambda b, s: (layer, 0, 0)
    prev = lambda b, s: (jnp.maximum((b * spb + s) * hpt - 1, 0), 0)
    nxt = lambda b, s: (jnp.minimum((b * spb + s + 1) * hpt, nhalo - 1), 0)
    in_specs = [
        pl.BlockSpec((HALO, D_MODEL), prev),
        pl.BlockSpec((tm, D_MODEL), lambda b, s: (b * spb + s, 0)),
        pl.BlockSpec((HALO, D_MODEL), nxt),
        _resident((None, D_MODEL, 2 * D_FF), lay3),
        _resident((None, CONV_W, 2 * D_FF), lay3),
        _resident((None, 1, 2 * D_FF), lay3),
        _resident((None, D_FF, D_MODEL), lay3),
        _resident((None, 1, D_MODEL), lay3), _resident((None, 1, D_MODEL), lay3),
    ]
    return pl.pallas_call(
        functools.partial(_ffn_kernel, alpha), grid=(batch, spb), in_specs=in_specs,
        out_specs=pl.BlockSpec((tm, D_MODEL), lambda b, s: (b * spb + s, 0)),
        out_shape=jax.ShapeDtypeStruct((n, D_MODEL), F32),
        scratch_shapes=[pltpu.VMEM((tm + 2 * HALO, D_MODEL), BF16),
                        pltpu.VMEM((2, tm + 2 * HALO, 2 * FF_CHUNK), F32)],
        compiler_params=_params(("parallel", "parallel")), name="conv_ffn",
    )(x, x, x, wu, ck, cb, wd, lg, lb)


def _rope_tables(seq):
    def table(pos_parts, half):
        inv = ROPE_THETA ** (-jnp.arange(half, dtype=F32) * (1.0 / half))
        cos, sin = [], []
        for pos in pos_parts:
            ang = pos.astype(F32)[:, None] * inv[None, :]
            cos += [jnp.cos(ang), jnp.cos(ang)]
            sin += [-jnp.sin(ang), jnp.sin(ang)]
        cos = jnp.concatenate(cos, axis=1)
        sin = jnp.concatenate(sin, axis=1)
        reps = LANES // cos.shape[1]
        return jnp.tile(cos, (1, reps)), jnp.tile(sin, (1, reps))

    pos = jnp.arange(seq, dtype=jnp.int32)
    cosa, sina = table([pos], HEAD_DIM // 2)
    cosb, sinb = table([pos // GRID_W, pos % GRID_W], HEAD_DIM // 4)
    return cosa, sina, cosb, sinb


def kernel(x, mem, w_in, b_gate, a_sink, b_q_gain, b_k_gain, c_ln_g, c_ln_b, c_ws, c_bs, w_branch, w_mix_out,
           ln1_g, ln1_b, x_wq, x_wkv, x_wo, ln2_g, ln2_b, f_w_up, f_conv_k, f_conv_b, f_w_down, ln3_g, ln3_b):
    batch, seq, _ = x.shape
    depth = w_in.shape[0]
    n = batch * seq
    alpha = (2 * depth) ** 0.25
    tm = min(ROW_TILE, seq)
    tm_ffn = min(FFN_ROW_TILE, seq)
    assert seq % tm == 0 and tm % CHUNK == 0 and seq % GRID_W == 0 and seq % tm_ffn == 0 and tm_ffn % HALO == 0
    assert D_FF % FF_CHUNK == 0
    nchunk = D_FF // FF_CHUNK

    w_main = w_in[:, :, 0:BRANCH_IN_W].astype(BF16)
    w_gate = w_in[:, :, BRANCH_IN_W:].astype(BF16)
    bg = b_gate.reshape(depth, 1, GATE_W)
    qg = jnp.tile(b_q_gain, (1, B_HEADS)).reshape(depth, 1, B_Q_W)
    kg = jnp.tile(b_k_gain, (1, B_KV)).reshape(depth, 1, B_KV_W)
    head_of = jnp.arange(B_Q_W, dtype=jnp.int32) // HEAD_DIM
    bd = (head_of[:, None] == head_of[None, :]).astype(BF16)
    lng = c_ln_g.reshape(depth, 1, C_WIDTH)
    lnb = c_ln_b.reshape(depth, 1, C_WIDTH)
    ws = c_ws.astype(BF16)
    bsb = jnp.broadcast_to(c_bs[:, :, :, None], (depth, C_GROUPS, CHUNK, C_WIDTH // C_GROUPS))
    wb = w_branch.astype(BF16)
    wo = w_mix_out.astype(BF16)
    wq = x_wq.astype(BF16)
    wkv = x_wkv.astype(BF16)
    xwo = x_wo.astype(BF16)

    wu = f_w_up.astype(BF16)
    ck = f_conv_k
    cb = f_conv_b.reshape(depth, 1, 2 * D_FF)
    wd = (0.5 * f_w_down).astype(BF16)
    vec = lambda a: a.reshape(depth, 1, D_MODEL)
    l1g, l1b, l2g, l2b, l3g, l3b = map(vec, (ln1_g, ln1_b, ln2_g, ln2_b, ln3_g, ln3_b))
    tabs = _rope_tables(seq)
    sink_rows = jnp.repeat(a_sink * LOG2E, BLOCK, axis=1).reshape(depth, A_KV, 1, (A_HEADS // A_KV) * BLOCK)

    mkt, mv = _memkv(mem, wkv)
    xs = x.reshape(n, D_MODEL)
    for l in range(depth):
        aqt, ak, avt, bqt, bk, bvt, oc = _inproj(xs, w_main, tabs, qg, kg, bd, lng, lnb, ws, bsb, l, batch, seq, tm)
        oa = _attn_a(sink_rows[l], aqt, ak, avt, batch, seq)
        ob = _attn_b(bqt, bk, bvt, batch, seq)
        xs = _merge(alpha, xs, oa, ob, oc, w_gate, bg, wb, wo, l1g, l1b, l, tm)
        xs = _xattn(alpha, xs, wq, mkt, mv, xwo, l2g, l2b, l, batch, seq, tm)
        xs = _ffn(alpha, xs, wu, ck, cb, wd, l3g, l3b, l, batch, seq, tm_ffn)
    return xs.reshape(batch, seq, D_MODEL)
```

```python
import functools
import math

import jax
import jax.numpy as jnp
from jax import lax
from jax.experimental import pallas as pl
from jax.experimental.pallas import tpu as pltpu

F32 = jnp.float32
BF16 = jnp.bfloat16

D_MODEL = 1024
HEAD_DIM = 64
BLOCK = 128
A_HEADS = 8
A_KV = 2
WINDOW = 128
B_HEADS = 8
B_KV = 2
GRID_W = 64
C_WIDTH = 512
C_GROUPS = 4
CHUNK = 128
N_BRANCH = 3
ROPE_THETA = 10000.0
X_HEADS = 4
X_HEAD_DIM = 128
X_WIDTH = X_HEADS * X_HEAD_DIM
D_FF = 2816
CONV_W = 3
LN_EPS = 1e-5
RMS_EPS = 1e-6

A_Q_W = A_HEADS * HEAD_DIM
A_KV_W = A_KV * HEAD_DIM
B_Q_W = B_HEADS * HEAD_DIM
B_KV_W = B_KV * HEAD_DIM
QKV_W = A_Q_W + 2 * A_KV_W
BRANCH_IN_W = 2 * QKV_W + 2 * C_WIDTH
GATE_W = N_BRANCH * D_MODEL

LANES = 128
ROW_TILE = 1024
FFN_ROW_TILE = 512
ATTN_B_TQ = 512
ATTN_A_QBLOCKS = 8
FF_CHUNK = 256
FF_DOWN_GROUP = 11
BF16_SUBLANES = 16
F32_SUBLANES = 8
HALO = BF16_SUBLANES
VMEM_LIMIT = 56 * 1024 * 1024
NEG_BIG = -1e30
LOG2E = 1.0 / math.log(2.0)


def _dot(a, b):
    return jnp.dot(a, b, preferred_element_type=F32)


def _layer_norm(y, g, b):
    mu = jnp.mean(y, axis=-1, keepdims=True)
    d = y - mu
    var = jnp.mean(d * d, axis=-1, keepdims=True)
    return d * lax.rsqrt(var + LN_EPS) * g + b


def _gelu(x):
    return 0.5 * x * (1.0 + lax.erf(x * (1.0 / math.sqrt(2.0))))


def _resident(shape, index_map):
    return pl.BlockSpec(shape, index_map, pipeline_mode=pl.Buffered(1))


def _params(sem):
    return pltpu.CompilerParams(dimension_semantics=sem, vmem_limit_bytes=VMEM_LIMIT)


def _inproj_kernel(x_ref, w_ref, cosa_ref, sina_ref, cosb_ref, sinb_ref, qg_ref, kg_ref, bd_ref,
                   lng_ref, lnb_ref, ws_ref, bsb_ref,
                   aqt_ref, ak_ref, avt_ref, bqt_ref, bk_ref, bvt_ref, oc_ref):
    tm = x_ref.shape[0]
    xb = x_ref[...].astype(BF16)
    lane = lax.broadcasted_iota(jnp.int32, (tm, LANES), 1)

    pa = _dot(xb, w_ref[:, 0:QKV_W])
    cosa = cosa_ref[...]
    sina = sina_ref[...]
    first_a = (lane & (HEAD_DIM - 1)) < (HEAD_DIM // 2)

    def rope_a(seg):
        sw = jnp.where(first_a, pltpu.roll(seg, LANES - HEAD_DIM // 2, 1), pltpu.roll(seg, HEAD_DIM // 2, 1))
        return seg * cosa + sw * sina

    scale = 1.0 / math.sqrt(HEAD_DIM)
    for j in range(A_Q_W // LANES):
        sl = slice(j * LANES, (j + 1) * LANES)
        aqt_ref[sl, :] = (rope_a(pa[:, sl]) * (scale * LOG2E)).T.astype(BF16)
    akr = rope_a(pa[:, A_Q_W:A_Q_W + A_KV_W]).astype(BF16)
    for g in range(A_KV):
        ak_ref[g] = akr[:, g * HEAD_DIM:(g + 1) * HEAD_DIM]
    avt_ref[...] = pa[:, A_Q_W + A_KV_W:QKV_W].T.astype(BF16)

    pb = _dot(xb, w_ref[:, QKV_W:2 * QKV_W])
    cosb = cosb_ref[...]
    sinb = sinb_ref[...]
    quarter = HEAD_DIM // 4
    first_b = (lane & (2 * quarter - 1)) < quarter

    def rope_b(seg):
        sw = jnp.where(first_b, pltpu.roll(seg, LANES - quarter, 1), pltpu.roll(seg, quarter, 1))
        return seg * cosb + sw * sinb

    def head_sumsq(v, bd):
        sq = v * v
        hi = sq.astype(BF16)
        lo = (sq - hi.astype(F32)).astype(BF16)
        return _dot(hi, bd) + _dot(lo, bd)

    bq = pb[:, 0:B_Q_W]
    bqn = bq * lax.rsqrt(head_sumsq(bq, bd_ref[...]) * (1.0 / HEAD_DIM) + RMS_EPS) * qg_ref[...]
    for j in range(B_Q_W // LANES):
        sl = slice(j * LANES, (j + 1) * LANES)
        bqt_ref[sl, :] = (rope_b(bqn[:, sl]) * (scale * LOG2E)).T.astype(BF16)
    bk = pb[:, B_Q_W:B_Q_W + B_KV_W]
    bkn = bk * lax.rsqrt(head_sumsq(bk, bd_ref[0:B_KV_W, 0:B_KV_W]) * (1.0 / HEAD_DIM) + RMS_EPS) * kg_ref[...]
    bkr = rope_b(bkn).astype(BF16)
    for g in range(B_KV):
        bk_ref[g] = bkr[:, g * HEAD_DIM:(g + 1) * HEAD_DIM]
    bvt_ref[...] = pb[:, B_Q_W + B_KV_W:QKV_W].T.astype(BF16)

    pc = _gelu(_dot(xb, w_ref[:, 2 * QKV_W:BRANCH_IN_W]))
    u = pc[:, 0:C_WIDTH]
    vln = _layer_norm(pc[:, C_WIDTH:2 * C_WIDTH], lng_ref[...], lnb_ref[...]).astype(BF16)
    nch = tm // CHUNK
    gw = C_WIDTH // C_GROUPS
    for g in range(C_GROUPS):
        cs = slice(g * gw, (g + 1) * gw)
        rhs = jnp.concatenate([vln[r * CHUNK:(r + 1) * CHUNK, cs] for r in range(nch)], axis=1)
        mixed = _dot(ws_ref[g], rhs)
        for r in range(nch):
            rs = slice(r * CHUNK, (r + 1) * CHUNK)
            oc_ref[rs, cs] = (u[rs, cs] * (mixed[:, r * gw:(r + 1) * gw] + bsb_ref[g])).astype(BF16)


def _inproj(x, w, tabs, qg, kg, bd, lng, lnb, ws, bsb, layer, batch, seq, tm):
    n = x.shape[0]
    spb = seq // tm
    row = lambda i: (i, 0)
    tab = lambda i: (i % spb, 0)
    lay3 = lambda i: (layer, 0, 0)
    lay4 = lambda i: (layer, 0, 0, 0)
    tr = lambda i: (i // spb, 0, i % spb)
    out_shape = (
        jax.ShapeDtypeStruct((batch, A_Q_W, seq), BF16),
        jax.ShapeDtypeStruct((A_KV, n, HEAD_DIM), BF16),
        jax.ShapeDtypeStruct((batch, A_KV_W, seq), BF16),
        jax.ShapeDtypeStruct((batch, B_Q_W, seq), BF16),
        jax.ShapeDtypeStruct((B_KV, n, HEAD_DIM), BF16),
        jax.ShapeDtypeStruct((batch, B_KV_W, seq), BF16),
        jax.ShapeDtypeStruct((n, C_WIDTH), BF16),
    )
    in_specs = [
        pl.BlockSpec((tm, D_MODEL), row),
        _resident((None, D_MODEL, BRANCH_IN_W), lay3),
        pl.BlockSpec((tm, LANES), tab), pl.BlockSpec((tm, LANES), tab),
        pl.BlockSpec((tm, LANES), tab), pl.BlockSpec((tm, LANES), tab),
        _resident((None, 1, B_Q_W), lay3), _resident((None, 1, B_KV_W), lay3),
        _resident((B_Q_W, B_Q_W), lambda i: (0, 0)),
        _resident((None, 1, C_WIDTH), lay3), _resident((None, 1, C_WIDTH), lay3),
        _resident((None, C_GROUPS, CHUNK, CHUNK), lay4),
        _resident((None, C_GROUPS, CHUNK, C_WIDTH // C_GROUPS), lay4),
    ]
    out_specs = (
        pl.BlockSpec((None, A_Q_W, tm), tr),
        pl.BlockSpec((A_KV, tm, HEAD_DIM), lambda i: (0, i, 0)),
        pl.BlockSpec((None, A_KV_W, tm), tr),
        pl.BlockSpec((None, B_Q_W, tm), tr),
        pl.BlockSpec((B_KV, tm, HEAD_DIM), lambda i: (0, i, 0)),
        pl.BlockSpec((None, B_KV_W, tm), tr),
        pl.BlockSpec((tm, C_WIDTH), row),
    )
    return pl.pallas_call(
        _inproj_kernel, grid=(n // tm,), in_specs=in_specs, out_specs=out_specs, out_shape=out_shape,
        compiler_params=_params(("parallel",)), name="inproj",
    )(x, w, *tabs, qg, kg, bd, lng, lnb, ws, bsb)


def _attn_a_kernel(sink_ref, qt_ref, kl_ref, kc_ref, kr_ref, vtl_ref, vtc_ref, vtr_ref, ones_ref, o_ref):
    qblocks = qt_ref.shape[1] // BLOCK
    group = A_HEADS // A_KV
    step = pl.program_id(1)
    last = pl.num_programs(1) - 1
    span = 3 * BLOCK
    kj = lax.broadcasted_iota(jnp.int32, (span, group * BLOCK), 0)
    qi = lax.broadcasted_iota(jnp.int32, (span, group * BLOCK), 1) & (BLOCK - 1)
    band = (kj >= qi) & (kj <= qi + 2 * WINDOW)
    units = [(g, t) for g in range(A_KV) for t in range(qblocks)]

    scores = []
    for g, t in units:
        kspan = jnp.concatenate([kl_ref[g], kc_ref[g], kr_ref[g]], axis=0)
        qt = jnp.concatenate(
            [qt_ref[(g * group + h) * HEAD_DIM:(g * group + h + 1) * HEAD_DIM, t * BLOCK:(t + 1) * BLOCK]
             for h in range(group)], axis=1)
        scores.append(_dot(kspan[t * BLOCK:t * BLOCK + span], qt))
    probs = []
    for (g, t), st in zip(units, scores):
        valid = band
        if t == 0:
            valid = valid & (kj >= jnp.where(step == 0, BLOCK, 0))
        if t == qblocks - 1:
            valid = valid & (kj < jnp.where(step == last, 2 * BLOCK, span))
        st = jnp.where(valid, st, NEG_BIG)
        m = jnp.maximum(jnp.max(st, axis=0, keepdims=True), sink_ref[g])
        probs.append((jnp.exp2(st - m).astype(BF16), jnp.exp2(sink_ref[g] - m)))
    for (g, t), (pt, sink_p) in zip(units, probs):
        rows = slice(g * HEAD_DIM, (g + 1) * HEAD_DIM)
        vts = jnp.concatenate([vtl_ref[rows, :], vtc_ref[rows, :], vtr_ref[rows, :]], axis=1)
        vte = jnp.concatenate([vts, ones_ref[...]], axis=0)
        oe = _dot(vte[:, t * BLOCK:t * BLOCK + span], pt)
        o = oe[0:HEAD_DIM] / (oe[HEAD_DIM:HEAD_DIM + 1] + sink_p)
        stacked = jnp.concatenate([o[:, h * BLOCK:(h + 1) * BLOCK] for h in range(group)], axis=0)
        o_ref[t * BLOCK:(t + 1) * BLOCK, g * group * HEAD_DIM:(g + 1) * group * HEAD_DIM] = (
            stacked.T.astype(BF16))


def _attn_a(sink_rows, qt, k, vt, batch, seq):
    n = batch * seq
    qb = ATTN_A_QBLOCKS if seq % (ATTN_A_QBLOCKS * BLOCK) == 0 else 1
    tq = qb * BLOCK
    steps = seq // tq
    nblk = seq // BLOCK
    left = lambda s: jnp.maximum(s * qb - 1, 0)
    right = lambda s: jnp.minimum((s + 1) * qb, nblk - 1)
    ones = (lax.broadcasted_iota(jnp.int32, (BF16_SUBLANES, tq + 2 * BLOCK), 0) == 0).astype(BF16)
    in_specs = [
        _resident(sink_rows.shape, lambda b, s: (0, 0, 0)),
        pl.BlockSpec((None, A_Q_W, tq), lambda b, s: (b, 0, s)),
        pl.BlockSpec((A_KV, BLOCK, HEAD_DIM), lambda b, s: (0, b * nblk + left(s), 0)),
        pl.BlockSpec((A_KV, tq, HEAD_DIM), lambda b, s: (0, b * steps + s, 0)),
        pl.BlockSpec((A_KV, BLOCK, HEAD_DIM), lambda b, s: (0, b * nblk + right(s), 0)),
        pl.BlockSpec((None, A_KV_W, BLOCK), lambda b, s: (b, 0, left(s))),
        pl.BlockSpec((None, A_KV_W, tq), lambda b, s: (b, 0, s)),
        pl.BlockSpec((None, A_KV_W, BLOCK), lambda b, s: (b, 0, right(s))),
        _resident(ones.shape, lambda b, s: (0, 0)),
    ]
    return pl.pallas_call(
        _attn_a_kernel, grid=(batch, steps), in_specs=in_specs,
        out_specs=pl.BlockSpec((tq, A_Q_W), lambda b, s: (b * steps + s, 0)),
        out_shape=jax.ShapeDtypeStruct((n, A_Q_W), BF16),
        compiler_params=_params(("parallel", "parallel")), name="attn_window",
    )(sink_rows, qt, k, k, k, vt, vt, vt, ones)


def _attn_b_kernel(qt_ref, k_ref, vt_ref, ones_ref, o_ref):
    group = B_HEADS // B_KV
    ones = ones_ref[...]

    def scores(head):
        return _dot(k_ref[head // group], qt_ref[head * HEAD_DIM:(head + 1) * HEAD_DIM, :])

    outs = []
    st_next = scores(0)
    for head in range(B_HEADS):
        st = st_next
        if head + 1 < B_HEADS:
            st_next = scores(head + 1)
        g = head // group
        vte = jnp.concatenate([vt_ref[g * HEAD_DIM:(g + 1) * HEAD_DIM, :], ones], axis=0)
        m = jnp.max(st, axis=0, keepdims=True)
        pt = jnp.exp2(st - m).astype(BF16)
        oe = _dot(vte, pt)
        outs.append(oe[0:HEAD_DIM] / oe[HEAD_DIM:HEAD_DIM + 1])
    o_ref[...] = jnp.concatenate(outs, axis=0).T.astype(BF16)


def _attn_b(qt, k, vt, batch, seq):
    n = batch * seq
    tq = min(ATTN_B_TQ, seq)
    steps = seq // tq
    ones = (lax.broadcasted_iota(jnp.int32, (BF16_SUBLANES, seq), 0) == 0).astype(BF16)
    in_specs = [
        pl.BlockSpec((None, B_Q_W, tq), lambda b, s: (b, 0, s)),
        pl.BlockSpec((B_KV, seq, HEAD_DIM), lambda b, s: (0, b, 0)),
        pl.BlockSpec((None, B_KV_W, seq), lambda b, s: (b, 0, 0)),
        _resident((BF16_SUBLANES, seq), lambda b, s: (0, 0)),
    ]
    return pl.pallas_call(
        _attn_b_kernel, grid=(batch, steps), in_specs=in_specs,
        out_specs=pl.BlockSpec((tq, B_Q_W), lambda b, s: (b * steps + s, 0)),
        out_shape=jax.ShapeDtypeStruct((n, B_Q_W), BF16),
        compiler_params=_params(("parallel", "parallel")), name="attn_dense",
    )(qt, k, vt, ones)


def _merge_kernel(alpha, x_ref, a_ref, b_ref, c_ref, wg_ref, bg_ref, wb_ref, wo_ref, g_ref, beta_ref, o_ref):
    x = x_ref[...]
    xb = x.astype(BF16)
    merged = None
    for i, br in enumerate((a_ref, b_ref, c_ref)):
        cols = slice(i * D_MODEL, (i + 1) * D_MODEL)
        gate = jax.nn.sigmoid(_dot(xb, wg_ref[:, cols]) + bg_ref[:, cols])
        t = _dot(br[...], wb_ref[i]) * gate
        merged = t if merged is None else merged + t
    h = _dot(merged.astype(BF16), wo_ref[...])
    o_ref[...] = _layer_norm(alpha * x + h, g_ref[...], beta_ref[...])


def _merge(alpha, x, oa, ob, oc, wg, bg, wb, wo, lg, lb, layer, tm):
    n = x.shape[0]
    row = lambda i: (i, 0)
    lay3 = lambda i: (layer, 0, 0)
    lay4 = lambda i: (layer, 0, 0, 0)
    in_specs = [
        pl.BlockSpec((tm, D_MODEL), row),
        pl.BlockSpec((tm, A_Q_W), row), pl.BlockSpec((tm, B_Q_W), row), pl.BlockSpec((tm, C_WIDTH), row),
        _resident((None, D_MODEL, GATE_W), lay3),
        _resident((None, 1, GATE_W), lay3),
        _resident((None, N_BRANCH, C_WIDTH, D_MODEL), lay4),
        _resident((None, D_MODEL, D_MODEL), lay3),
        _resident((None, 1, D_MODEL), lay3), _resident((None, 1, D_MODEL), lay3),
    ]
    return pl.pallas_call(
        functools.partial(_merge_kernel, alpha), grid=(n // tm,), in_specs=in_specs,
        out_specs=pl.BlockSpec((tm, D_MODEL), row),
        out_shape=jax.ShapeDtypeStruct((n, D_MODEL), F32),
        compiler_params=_params(("parallel",)), name="merge",
    )(x, oa, ob, oc, wg, bg, wb, wo, lg, lb)


def _memkv_kernel(m_ref, w_ref, kt_ref, v_ref):
    kv = _dot(m_ref[...].astype(BF16), w_ref[...])
    kt_ref[...] = kv[:, 0:X_WIDTH].T.astype(BF16)
    v_ref[...] = kv[:, X_WIDTH:2 * X_WIDTH].astype(BF16)


def _memkv(mem, wkv):
    batch, mlen, _ = mem.shape
    depth = wkv.shape[0]
    return pl.pallas_call(
        _memkv_kernel, grid=(depth, batch),
        in_specs=[pl.BlockSpec((None, mlen, D_MODEL), lambda l, b: (b, 0, 0)),
                  pl.BlockSpec((None, D_MODEL, 2 * X_WIDTH), lambda l, b: (l, 0, 0))],
        out_specs=(pl.BlockSpec((None, None, X_WIDTH, mlen), lambda l, b: (l, b, 0, 0)),
                   pl.BlockSpec((None, None, mlen, X_WIDTH), lambda l, b: (l, b, 0, 0))),
        out_shape=(jax.ShapeDtypeStruct((depth, batch, X_WIDTH, mlen), BF16),
                   jax.ShapeDtypeStruct((depth, batch, mlen, X_WIDTH), BF16)),
        compiler_params=_params(("parallel", "parallel")), name="mem_kv",
    )(mem, wkv)


def _xattn_kernel(alpha, x_ref, wq_ref, kt_ref, v_ref, wo_ref, g_ref, beta_ref, o_ref):
    x = x_ref[...]
    q = (_dot(x.astype(BF16), wq_ref[...]) * (LOG2E / math.sqrt(X_HEAD_DIM))).astype(BF16)
    heads = [slice(h * X_HEAD_DIM, (h + 1) * X_HEAD_DIM) for h in range(X_HEADS)]
    scores = [_dot(q[:, cols], kt_ref[cols, :]) for cols in heads]
    probs = []
    for s in scores:
        p = jnp.exp2(s - jnp.max(s, axis=-1, keepdims=True))
        probs.append((p.astype(BF16), jnp.sum(p, axis=-1, keepdims=True)))
    outs = [_dot(p, v_ref[:, cols]) / denom for cols, (p, denom) in zip(heads, probs)]
    att = jnp.concatenate(outs, axis=1).astype(BF16)
    h_out = _dot(att, wo_ref[...])
    o_ref[...] = _layer_norm(alpha * x + h_out, g_ref[...], beta_ref[...])


def _xattn(alpha, x, wq, kt, v, wo, lg, lb, layer, batch, seq, tm):
    n = x.shape[0]
    spb = seq // tm
    mlen = kt.shape[-1]
    lay3 = lambda b, s: (layer, 0, 0)
    in_specs = [
        pl.BlockSpec((tm, D_MODEL), lambda b, s: (b * spb + s, 0)),
        _resident((None, D_MODEL, X_WIDTH), lay3),
        pl.BlockSpec((None, None, X_WIDTH, mlen), lambda b, s: (layer, b, 0, 0)),
        pl.BlockSpec((None, None, mlen, X_WIDTH), lambda b, s: (layer, b, 0, 0)),
        _resident((None, X_WIDTH, D_MODEL), lay3),
        _resident((None, 1, D_MODEL), lay3), _resident((None, 1, D_MODEL), lay3),
    ]
    return pl.pallas_call(
        functools.partial(_xattn_kernel, alpha), grid=(batch, spb), in_specs=in_specs,
        out_specs=pl.BlockSpec((tm, D_MODEL), lambda b, s: (b * spb + s, 0)),
        out_shape=jax.ShapeDtypeStruct((n, D_MODEL), F32),
        compiler_params=_params(("parallel", "parallel")), name="cross_attn",
    )(x, wq, kt, v, wo, lg, lb)


def _ffn_kernel(alpha, xp_ref, x_ref, xn_ref, wu_ref, ck_ref, cb_ref, wd_ref, g_ref, beta_ref, o_ref,
                xe_ref, h_ref):
    tm = x_ref.shape[0]
    step = pl.program_id(1)
    last = pl.num_programs(1) - 1
    x = x_ref[...]
    nxt_row = jnp.where(step == last, 0.0, xn_ref[0:1, :])
    prv_row = jnp.where(step == 0, 0.0, xp_ref[F32_SUBLANES - 1:F32_SUBLANES, :])
    halo = jnp.concatenate([nxt_row, jnp.zeros((HALO - 2, D_MODEL), F32), prv_row], axis=0)
    xe_ref[0:tm, :] = x.astype(BF16)
    xe_ref[tm:tm + HALO, :] = halo.astype(BF16)
    nchunk = D_FF // FF_CHUNK

    def halves(c):
        return (slice(c * FF_CHUNK, (c + 1) * FF_CHUNK), slice(D_FF + c * FF_CHUNK, D_FF + (c + 1) * FF_CHUNK))

    def up(c):
        for i, cols in enumerate(halves(c)):
            h_ref[c % 2, :, i * FF_CHUNK:(i + 1) * FF_CHUNK] = _dot(xe_ref[...], wu_ref[:, cols])

    def gate(c):
        hs = h_ref.at[c % 2]
        conv = []
        for i, cols in enumerate(halves(c)):
            ck = ck_ref[:, cols]
            lanes = slice(i * FF_CHUNK, (i + 1) * FF_CHUNK)
            before = jnp.concatenate([hs[tm + HALO - 1:tm + HALO, lanes], hs[0:tm - 1, lanes]], axis=0)
            after = jnp.concatenate([hs[1:tm, lanes], hs[tm:tm + 1, lanes]], axis=0)
            conv.append(before * ck[0:1] + hs[0:tm, lanes] * ck[1:2] + after * ck[2:3] + cb_ref[:, cols])
        a = conv[0]
        return (a * (1.0 + lax.erf(a * (1.0 / math.sqrt(2.0)))) * conv[1]).astype(BF16)

    up(0)
    acc = None
    acts = []
    for c in range(nchunk):
        if c + 1 < nchunk:
            up(c + 1)
        acts.append(gate(c))
        if len(acts) == FF_DOWN_GROUP or c + 1 == nchunk:
            first = c + 1 - len(acts)
            act = acts[0] if len(acts) == 1 else jnp.concatenate(acts, axis=1)
            d = _dot(act, wd_ref[first * FF_CHUNK:(c + 1) * FF_CHUNK, :])
            acc = d if acc is None else acc + d
            acts = []
    o_ref[...] = _layer_norm(alpha * x + acc, g_ref[...], beta_ref[...])


def _ffn(alpha, x, wu, ck, cb, wd, lg, lb, layer, batch, seq, tm):
    n = x.shape[0]
    spb = seq // tm
    hpt = tm // F32_SUBLANES
    nhalo = n // F32_SUBLANES
    lay3 = lambda b, s: (layer, 0, 0)
    prev = lambda b, s: (jnp.maximum((b * spb + s) * hpt - 1, 0), 0)
    nxt = lambda b, s: (jnp.minimum((b * spb + s + 1) * hpt, nhalo - 1), 0)
    in_specs = [
        pl.BlockSpec((F32_SUBLANES, D_MODEL), prev),
        pl.BlockSpec((tm, D_MODEL), lambda b, s: (b * spb + s, 0)),
        pl.BlockSpec((F32_SUBLANES, D_MODEL), nxt),
        _resident((None, D_MODEL, 2 * D_FF), lay3),
        _resident((None, CONV_W, 2 * D_FF), lay3),
        _resident((None, 1, 2 * D_FF), lay3),
        _resident((None, D_FF, D_MODEL), lay3),
        _resident((None, 1, D_MODEL), lay3), _resident((None, 1, D_MODEL), lay3),
    ]
    return pl.pallas_call(
        functools.partial(_ffn_kernel, alpha), grid=(batch, spb), in_specs=in_specs,
        out_specs=pl.BlockSpec((tm, D_MODEL), lambda b, s: (b * spb + s, 0)),
        out_shape=jax.ShapeDtypeStruct((n, D_MODEL), F32),
        scratch_shapes=[pltpu.VMEM((tm + HALO, D_MODEL), BF16),
                        pltpu.VMEM((2, tm + HALO, 2 * FF_CHUNK), F32)],
        compiler_params=_params(("parallel", "parallel")), name="conv_ffn",
    )(x, x, x, wu, ck, cb, wd, lg, lb)


def _rope_tables(seq):
    def table(pos_parts, half):
        inv = ROPE_THETA ** (-jnp.arange(half, dtype=F32) * (1.0 / half))
        cos, sin = [], []
        for pos in pos_parts:
            ang = pos.astype(F32)[:, None] * inv[None, :]
            cos += [jnp.cos(ang), jnp.cos(ang)]
            sin += [-jnp.sin(ang), jnp.sin(ang)]
        cos = jnp.concatenate(cos, axis=1)
        sin = jnp.concatenate(sin, axis=1)
        reps = LANES // cos.shape[1]
        return jnp.tile(cos, (1, reps)), jnp.tile(sin, (1, reps))

    pos = jnp.arange(seq, dtype=jnp.int32)
    cosa, sina = table([pos], HEAD_DIM // 2)
    cosb, sinb = table([pos // GRID_W, pos % GRID_W], HEAD_DIM // 4)
    return cosa, sina, cosb, sinb


def kernel(x, mem, w_in, b_gate, a_sink, b_q_gain, b_k_gain, c_ln_g, c_ln_b, c_ws, c_bs, w_branch, w_mix_out,
           ln1_g, ln1_b, x_wq, x_wkv, x_wo, ln2_g, ln2_b, f_w_up, f_conv_k, f_conv_b, f_w_down, ln3_g, ln3_b):
    batch, seq, _ = x.shape
    depth = w_in.shape[0]
    n = batch * seq
    alpha = (2 * depth) ** 0.25
    tm = min(ROW_TILE, seq)
    tm_ffn = min(FFN_ROW_TILE, seq)
    assert seq % tm == 0 and tm % CHUNK == 0 and seq % GRID_W == 0 and seq % tm_ffn == 0 and tm_ffn % HALO == 0
    assert D_FF % FF_CHUNK == 0
    nchunk = D_FF // FF_CHUNK

    w_main = w_in[:, :, 0:BRANCH_IN_W].astype(BF16)
    w_gate = w_in[:, :, BRANCH_IN_W:].astype(BF16)
    bg = b_gate.reshape(depth, 1, GATE_W)
    qg = jnp.tile(b_q_gain, (1, B_HEADS)).reshape(depth, 1, B_Q_W)
    kg = jnp.tile(b_k_gain, (1, B_KV)).reshape(depth, 1, B_KV_W)
    head_of = jnp.arange(B_Q_W, dtype=jnp.int32) // HEAD_DIM
    bd = (head_of[:, None] == head_of[None, :]).astype(BF16)
    lng = c_ln_g.reshape(depth, 1, C_WIDTH)
    lnb = c_ln_b.reshape(depth, 1, C_WIDTH)
    ws = c_ws.astype(BF16)
    bsb = jnp.broadcast_to(c_bs[:, :, :, None], (depth, C_GROUPS, CHUNK, C_WIDTH // C_GROUPS))
    wb = w_branch.astype(BF16)
    wo = w_mix_out.astype(BF16)
    wq = x_wq.astype(BF16)
    wkv = x_wkv.astype(BF16)
    xwo = x_wo.astype(BF16)

    wu = f_w_up.astype(BF16)
    ck = f_conv_k
    cb = f_conv_b.reshape(depth, 1, 2 * D_FF)
    wd = (0.5 * f_w_down).astype(BF16)
    vec = lambda a: a.reshape(depth, 1, D_MODEL)
    l1g, l1b, l2g, l2b, l3g, l3b = map(vec, (ln1_g, ln1_b, ln2_g, ln2_b, ln3_g, ln3_b))
    tabs = _rope_tables(seq)
    sink_rows = jnp.repeat(a_sink * LOG2E, BLOCK, axis=1).reshape(depth, A_KV, 1, (A_HEADS // A_KV) * BLOCK)

    mkt, mv = _memkv(mem, wkv)
    xs = x.reshape(n, D_MODEL)
    for l in range(depth):
        aqt, ak, avt, bqt, bk, bvt, oc = _inproj(xs, w_main, tabs, qg, kg, bd, lng, lnb, ws, bsb, l, batch, seq, tm)
        oa = _attn_a(sink_rows[l], aqt, ak, avt, batch, seq)
        ob = _attn_b(bqt, bk, bvt, batch, seq)
        xs = _merge(alpha, xs, oa, ob, oc, w_gate, bg, wb, wo, l1g, l1b, l, tm)
        xs = _xattn(alpha, xs, wq, mkt, mv, xwo, l2g, l2b, l, batch, seq, tm)
        xs = _ffn(alpha, xs, wu, ck, cb, wd, l3g, l3b, l, batch, seq, tm_ffn)
    return xs.reshape(batch, seq, D_MODEL)
```

```python
import functools
import math

import jax
import jax.numpy as jnp
from jax import lax
from jax.experimental import pallas as pl
from jax.experimental.pallas import tpu as pltpu

F32 = jnp.float32
BF16 = jnp.bfloat16

D_MODEL = 1024
HEAD_DIM = 64
BLOCK = 128
A_HEADS = 8
A_KV = 2
WINDOW = 128
B_HEADS = 8
B_KV = 2
GRID_W = 64
C_WIDTH = 512
C_GROUPS = 4
CHUNK = 128
N_BRANCH = 3
ROPE_THETA = 10000.0
X_HEADS = 4
X_HEAD_DIM = 128
X_WIDTH = X_HEADS * X_HEAD_DIM
D_FF = 2816
CONV_W = 3
LN_EPS = 1e-5
RMS_EPS = 1e-6

A_Q_W = A_HEADS * HEAD_DIM
A_KV_W = A_KV * HEAD_DIM
B_Q_W = B_HEADS * HEAD_DIM
B_KV_W = B_KV * HEAD_DIM
QKV_W = A_Q_W + 2 * A_KV_W
BRANCH_IN_W = 2 * QKV_W + 2 * C_WIDTH
GATE_W = N_BRANCH * D_MODEL

LANES = 128
ROW_TILE = 1024
FFN_ROW_TILE = 512
ATTN_B_TQ = 512
ATTN_B_SUBTILES = 2
ATTN_A_QBLOCKS = 8
FF_CHUNK = 256
FF_DOWN_GROUP = 11
BF16_SUBLANES = 16
F32_SUBLANES = 8
HALO = BF16_SUBLANES
VMEM_LIMIT = 56 * 1024 * 1024
NEG_BIG = -1e30
LOG2E = 1.0 / math.log(2.0)


def _dot(a, b):
    return jnp.dot(a, b, preferred_element_type=F32)


def _layer_norm(y, g, b):
    mu = jnp.mean(y, axis=-1, keepdims=True)
    d = y - mu
    var = jnp.mean(d * d, axis=-1, keepdims=True)
    return d * lax.rsqrt(var + LN_EPS) * g + b


def _gelu(x):
    return 0.5 * x * (1.0 + lax.erf(x * (1.0 / math.sqrt(2.0))))


def _resident(shape, index_map):
    return pl.BlockSpec(shape, index_map, pipeline_mode=pl.Buffered(1))


def _params(sem):
    return pltpu.CompilerParams(dimension_semantics=sem, vmem_limit_bytes=VMEM_LIMIT)


def _inproj_kernel(x_ref, w_ref, cosa_ref, sina_ref, cosb_ref, sinb_ref, cat_ref, sat_ref, cbt_ref, sbt_ref,
                   kg_ref, bd_ref, lng_ref, lnb_ref, ws_ref, bsb_ref,
                   aqt_ref, ak_ref, avt_ref, bqt_ref, bk_ref, bvt_ref, oc_ref):
    tm = x_ref.shape[0]
    xb = x_ref[...].astype(BF16)
    lane = lax.broadcasted_iota(jnp.int32, (tm, LANES), 1)

    pa = _dot(xb, w_ref[:, 0:QKV_W])
    cosa = cosa_ref[...]
    sina = sina_ref[...]
    first_a = (lane & (HEAD_DIM - 1)) < (HEAD_DIM // 2)

    def rope_a(seg):
        sw = jnp.where(first_a, pltpu.roll(seg, LANES - HEAD_DIM // 2, 1), pltpu.roll(seg, HEAD_DIM // 2, 1))
        return seg * cosa + sw * sina

    half = HEAD_DIM // 2
    cat = cat_ref[...]
    sat = sat_ref[...]
    for j in range(A_Q_W // LANES):
        t = pa[:, j * LANES:(j + 1) * LANES].T
        for hh in range(LANES // HEAD_DIM):
            th = t[hh * HEAD_DIM:(hh + 1) * HEAD_DIM]
            sw = jnp.concatenate([th[half:], th[:half]], axis=0)
            head = j * (LANES // HEAD_DIM) + hh
            aqt_ref[head * HEAD_DIM:(head + 1) * HEAD_DIM, :] = (th * cat + sw * sat).astype(BF16)
    akr = rope_a(pa[:, A_Q_W:A_Q_W + A_KV_W]).astype(BF16)
    for g in range(A_KV):
        ak_ref[g] = akr[:, g * HEAD_DIM:(g + 1) * HEAD_DIM]
    avt_ref[...] = pa[:, A_Q_W + A_KV_W:QKV_W].T.astype(BF16)

    pb = _dot(xb, w_ref[:, QKV_W:2 * QKV_W])
    cosb = cosb_ref[...]
    sinb = sinb_ref[...]
    quarter = HEAD_DIM // 4
    first_b = (lane & (2 * quarter - 1)) < quarter

    def rope_b(seg):
        sw = jnp.where(first_b, pltpu.roll(seg, LANES - quarter, 1), pltpu.roll(seg, quarter, 1))
        return seg * cosb + sw * sinb

    def head_sumsq(v, bd):
        sq = v * v
        hi = sq.astype(BF16)
        lo = (sq - hi.astype(F32)).astype(BF16)
        return _dot(hi, bd) + _dot(lo, bd)

    cbt = cbt_ref[...]
    sbt = sbt_ref[...]
    for j in range(B_Q_W // LANES):
        t = pb[:, j * LANES:(j + 1) * LANES].T
        for hh in range(LANES // HEAD_DIM):
            th = t[hh * HEAD_DIM:(hh + 1) * HEAD_DIM]
            inv = lax.rsqrt(jnp.sum(th * th, axis=0, keepdims=True) * (1.0 / HEAD_DIM) + RMS_EPS)
            sw = jnp.concatenate([th[quarter:2 * quarter], th[:quarter], th[3 * quarter:], th[2 * quarter:3 * quarter]],
                                 axis=0)
            head = j * (LANES // HEAD_DIM) + hh
            bqt_ref[head * HEAD_DIM:(head + 1) * HEAD_DIM, :] = ((th * cbt + sw * sbt) * inv).astype(BF16)
    bk = pb[:, B_Q_W:B_Q_W + B_KV_W]
    bkn = bk * lax.rsqrt(head_sumsq(bk, bd_ref[...]) * (1.0 / HEAD_DIM) + RMS_EPS) * kg_ref[...]
    bkr = rope_b(bkn).astype(BF16)
    for g in range(B_KV):
        bk_ref[g] = bkr[:, g * HEAD_DIM:(g + 1) * HEAD_DIM]
    bvt_ref[...] = pb[:, B_Q_W + B_KV_W:QKV_W].T.astype(BF16)

    pc = _gelu(_dot(xb, w_ref[:, 2 * QKV_W:BRANCH_IN_W]))
    u = pc[:, 0:C_WIDTH]
    vln = _layer_norm(pc[:, C_WIDTH:2 * C_WIDTH], lng_ref[...], lnb_ref[...]).astype(BF16)
    nch = tm // CHUNK
    gw = C_WIDTH // C_GROUPS
    for g in range(C_GROUPS):
        cs = slice(g * gw, (g + 1) * gw)
        rhs = jnp.concatenate([vln[r * CHUNK:(r + 1) * CHUNK, cs] for r in range(nch)], axis=1)
        mixed = _dot(ws_ref[g], rhs)
        for r in range(nch):
            rs = slice(r * CHUNK, (r + 1) * CHUNK)
            oc_ref[rs, cs] = (u[rs, cs] * (mixed[:, r * gw:(r + 1) * gw] + bsb_ref[g])).astype(BF16)


def _inproj(x, w, tabs, ttabs, kg, bd, lng, lnb, ws, bsb, layer, batch, seq, tm):
    n = x.shape[0]
    spb = seq // tm
    row = lambda i: (i, 0)
    tab = lambda i: (i % spb, 0)
    lay3 = lambda i: (layer, 0, 0)
    lay4 = lambda i: (layer, 0, 0, 0)
    tr = lambda i: (i // spb, 0, i % spb)
    out_shape = (
        jax.ShapeDtypeStruct((batch, A_Q_W, seq), BF16),
        jax.ShapeDtypeStruct((A_KV, n, HEAD_DIM), BF16),
        jax.ShapeDtypeStruct((batch, A_KV_W, seq), BF16),
        jax.ShapeDtypeStruct((batch, B_Q_W, seq), BF16),
        jax.ShapeDtypeStruct((B_KV, n, HEAD_DIM), BF16),
        jax.ShapeDtypeStruct((batch, B_KV_W, seq), BF16),
        jax.ShapeDtypeStruct((n, C_WIDTH), BF16),
    )
    in_specs = [
        pl.BlockSpec((tm, D_MODEL), row),
        _resident((None, D_MODEL, BRANCH_IN_W), lay3),
        pl.BlockSpec((tm, LANES), tab), pl.BlockSpec((tm, LANES), tab),
        pl.BlockSpec((tm, LANES), tab), pl.BlockSpec((tm, LANES), tab),
        pl.BlockSpec((HEAD_DIM, tm), lambda i: (0, i % spb)), pl.BlockSpec((HEAD_DIM, tm), lambda i: (0, i % spb)),
        pl.BlockSpec((None, HEAD_DIM, tm), lambda i: (layer, 0, i % spb)),
        pl.BlockSpec((None, HEAD_DIM, tm), lambda i: (layer, 0, i % spb)),
        _resident((None, 1, B_KV_W), lay3),
        _resident((B_KV_W, B_KV_W), lambda i: (0, 0)),
        _resident((None, 1, C_WIDTH), lay3), _resident((None, 1, C_WIDTH), lay3),
        _resident((None, C_GROUPS, CHUNK, CHUNK), lay4),
        _resident((None, C_GROUPS, CHUNK, C_WIDTH // C_GROUPS), lay4),
    ]
    out_specs = (
        pl.BlockSpec((None, A_Q_W, tm), tr),
        pl.BlockSpec((A_KV, tm, HEAD_DIM), lambda i: (0, i, 0)),
        pl.BlockSpec((None, A_KV_W, tm), tr),
        pl.BlockSpec((None, B_Q_W, tm), tr),
        pl.BlockSpec((B_KV, tm, HEAD_DIM), lambda i: (0, i, 0)),
        pl.BlockSpec((None, B_KV_W, tm), tr),
        pl.BlockSpec((tm, C_WIDTH), row),
    )
    return pl.pallas_call(
        _inproj_kernel, grid=(n // tm,), in_specs=in_specs, out_specs=out_specs, out_shape=out_shape,
        compiler_params=_params(("parallel",)), name="inproj",
    )(x, w, *tabs, *ttabs, kg, bd, lng, lnb, ws, bsb)


def _attn_a_kernel(sink_ref, qt_ref, kl_ref, kc_ref, kr_ref, vtl_ref, vtc_ref, vtr_ref, ones_ref, o_ref):
    qblocks = qt_ref.shape[1] // BLOCK
    group = A_HEADS // A_KV
    step = pl.program_id(1)
    last = pl.num_programs(1) - 1
    span = 3 * BLOCK
    kj = lax.broadcasted_iota(jnp.int32, (span, group * BLOCK), 0)
    qi = lax.broadcasted_iota(jnp.int32, (span, group * BLOCK), 1) & (BLOCK - 1)
    band = (kj >= qi) & (kj <= qi + 2 * WINDOW)
    units = [(g, t) for g in range(A_KV) for t in range(qblocks)]

    scores = []
    for g, t in units:
        kspan = jnp.concatenate([kl_ref[g], kc_ref[g], kr_ref[g]], axis=0)
        qt = jnp.concatenate(
            [qt_ref[(g * group + h) * HEAD_DIM:(g * group + h + 1) * HEAD_DIM, t * BLOCK:(t + 1) * BLOCK]
             for h in range(group)], axis=1)
        scores.append(_dot(kspan[t * BLOCK:t * BLOCK + span], qt))
    probs = []
    for (g, t), st in zip(units, scores):
        valid = band
        if t == 0:
            valid = valid & (kj >= jnp.where(step == 0, BLOCK, 0))
        if t == qblocks - 1:
            valid = valid & (kj < jnp.where(step == last, 2 * BLOCK, span))
        st = jnp.where(valid, st, NEG_BIG)
        m = jnp.maximum(jnp.max(st, axis=0, keepdims=True), sink_ref[g])
        probs.append((jnp.exp2(st - m).astype(BF16), jnp.exp2(sink_ref[g] - m)))
    for (g, t), (pt, sink_p) in zip(units, probs):
        rows = slice(g * HEAD_DIM, (g + 1) * HEAD_DIM)
        vts = jnp.concatenate([vtl_ref[rows, :], vtc_ref[rows, :], vtr_ref[rows, :]], axis=1)
        vte = jnp.concatenate([vts, ones_ref[...]], axis=0)
        oe = _dot(vte[:, t * BLOCK:t * BLOCK + span], pt)
        o = oe[0:HEAD_DIM] / (oe[HEAD_DIM:HEAD_DIM + 1] + sink_p)
        stacked = jnp.concatenate([o[:, h * BLOCK:(h + 1) * BLOCK] for h in range(group)], axis=0)
        o_ref[t * BLOCK:(t + 1) * BLOCK, g * group * HEAD_DIM:(g + 1) * group * HEAD_DIM] = (
            stacked.T.astype(BF16))


def _attn_a(sink_rows, qt, k, vt, batch, seq):
    n = batch * seq
    qb = ATTN_A_QBLOCKS if seq % (ATTN_A_QBLOCKS * BLOCK) == 0 else 1
    tq = qb * BLOCK
    steps = seq // tq
    nblk = seq // BLOCK
    left = lambda s: jnp.maximum(s * qb - 1, 0)
    right = lambda s: jnp.minimum((s + 1) * qb, nblk - 1)
    ones = (lax.broadcasted_iota(jnp.int32, (BF16_SUBLANES, tq + 2 * BLOCK), 0) == 0).astype(BF16)
    in_specs = [
        _resident(sink_rows.shape, lambda b, s: (0, 0, 0)),
        pl.BlockSpec((None, A_Q_W, tq), lambda b, s: (b, 0, s)),
        pl.BlockSpec((A_KV, BLOCK, HEAD_DIM), lambda b, s: (0, b * nblk + left(s), 0)),
        pl.BlockSpec((A_KV, tq, HEAD_DIM), lambda b, s: (0, b * steps + s, 0)),
        pl.BlockSpec((A_KV, BLOCK, HEAD_DIM), lambda b, s: (0, b * nblk + right(s), 0)),
        pl.BlockSpec((None, A_KV_W, BLOCK), lambda b, s: (b, 0, left(s))),
        pl.BlockSpec((None, A_KV_W, tq), lambda b, s: (b, 0, s)),
        pl.BlockSpec((None, A_KV_W, BLOCK), lambda b, s: (b, 0, right(s))),
        _resident(ones.shape, lambda b, s: (0, 0)),
    ]
    return pl.pallas_call(
        _attn_a_kernel, grid=(batch, steps), in_specs=in_specs,
        out_specs=pl.BlockSpec((tq, A_Q_W), lambda b, s: (b * steps + s, 0)),
        out_shape=jax.ShapeDtypeStruct((n, A_Q_W), BF16),
        compiler_params=_params(("parallel", "parallel")), name="attn_window",
    )(sink_rows, qt, k, k, k, vt, vt, vt, ones)


def _attn_b_kernel(qt_ref, k_ref, vt_ref, ones_ref, o_ref):
    group = B_HEADS // B_KV
    ones = ones_ref[...]
    tq = min(ATTN_B_TQ, qt_ref.shape[1])
    units = [(sub, head) for sub in range(qt_ref.shape[1] // tq) for head in range(B_HEADS)]

    def scores(u):
        sub, head = units[u]
        return _dot(k_ref[head // group],
                    qt_ref[head * HEAD_DIM:(head + 1) * HEAD_DIM, sub * tq:(sub + 1) * tq])

    outs = []
    st_next = scores(0)
    for u, (sub, head) in enumerate(units):
        st = st_next
        if u + 1 < len(units):
            st_next = scores(u + 1)
        g = head // group
        vte = jnp.concatenate([vt_ref[g * HEAD_DIM:(g + 1) * HEAD_DIM, :], ones], axis=0)
        m = jnp.max(st, axis=0, keepdims=True)
        pt = jnp.exp2(st - m).astype(BF16)
        oe = _dot(vte, pt)
        outs.append(oe[0:HEAD_DIM] / oe[HEAD_DIM:HEAD_DIM + 1])
        if head == B_HEADS - 1:
            o_ref[sub * tq:(sub + 1) * tq, :] = jnp.concatenate(outs, axis=0).T.astype(BF16)
            outs = []


def _attn_b(qt, k, vt, batch, seq):
    n = batch * seq
    tq = min(ATTN_B_TQ * ATTN_B_SUBTILES, seq)
    steps = seq // tq
    ones = (lax.broadcasted_iota(jnp.int32, (BF16_SUBLANES, seq), 0) == 0).astype(BF16)
    in_specs = [
        pl.BlockSpec((None, B_Q_W, tq), lambda b, s: (b, 0, s)),
        pl.BlockSpec((B_KV, seq, HEAD_DIM), lambda b, s: (0, b, 0)),
        pl.BlockSpec((None, B_KV_W, seq), lambda b, s: (b, 0, 0)),
        _resident((BF16_SUBLANES, seq), lambda b, s: (0, 0)),
    ]
    return pl.pallas_call(
        _attn_b_kernel, grid=(batch, steps), in_specs=in_specs,
        out_specs=pl.BlockSpec((tq, B_Q_W), lambda b, s: (b * steps + s, 0)),
        out_shape=jax.ShapeDtypeStruct((n, B_Q_W), BF16),
        compiler_params=_params(("parallel", "parallel")), name="attn_dense",
    )(qt, k, vt, ones)


def _merge_kernel(alpha, x_ref, a_ref, b_ref, c_ref, wg_ref, bg_ref, wb_ref, wo_ref, g_ref, beta_ref, o_ref):
    x = x_ref[...]
    xb = x.astype(BF16)
    merged = None
    for i, br in enumerate((a_ref, b_ref, c_ref)):
        cols = slice(i * D_MODEL, (i + 1) * D_MODEL)
        gate = jax.nn.sigmoid(_dot(xb, wg_ref[:, cols]) + bg_ref[:, cols])
        t = _dot(br[...], wb_ref[i]) * gate
        merged = t if merged is None else merged + t
    h = _dot(merged.astype(BF16), wo_ref[...])
    o_ref[...] = _layer_norm(alpha * x + h, g_ref[...], beta_ref[...])


def _merge(alpha, x, oa, ob, oc, wg, bg, wb, wo, lg, lb, layer, tm):
    n = x.shape[0]
    row = lambda i: (i, 0)
    lay3 = lambda i: (layer, 0, 0)
    lay4 = lambda i: (layer, 0, 0, 0)
    in_specs = [
        pl.BlockSpec((tm, D_MODEL), row),
        pl.BlockSpec((tm, A_Q_W), row), pl.BlockSpec((tm, B_Q_W), row), pl.BlockSpec((tm, C_WIDTH), row),
        _resident((None, D_MODEL, GATE_W), lay3),
        _resident((None, 1, GATE_W), lay3),
        _resident((None, N_BRANCH, C_WIDTH, D_MODEL), lay4),
        _resident((None, D_MODEL, D_MODEL), lay3),
        _resident((None, 1, D_MODEL), lay3), _resident((None, 1, D_MODEL), lay3),
    ]
    return pl.pallas_call(
        functools.partial(_merge_kernel, alpha), grid=(n // tm,), in_specs=in_specs,
        out_specs=pl.BlockSpec((tm, D_MODEL), row),
        out_shape=jax.ShapeDtypeStruct((n, D_MODEL), F32),
        compiler_params=_params(("parallel",)), name="merge",
    )(x, oa, ob, oc, wg, bg, wb, wo, lg, lb)


def _memkv_kernel(m_ref, w_ref, kt_ref, v_ref):
    kv = _dot(m_ref[...].astype(BF16), w_ref[...])
    kt_ref[...] = kv[:, 0:X_WIDTH].T.astype(BF16)
    v_ref[...] = kv[:, X_WIDTH:2 * X_WIDTH].astype(BF16)


def _memkv(mem, wkv):
    batch, mlen, _ = mem.shape
    depth = wkv.shape[0]
    return pl.pallas_call(
        _memkv_kernel, grid=(depth, batch),
        in_specs=[pl.BlockSpec((None, mlen, D_MODEL), lambda l, b: (b, 0, 0)),
                  pl.BlockSpec((None, D_MODEL, 2 * X_WIDTH), lambda l, b: (l, 0, 0))],
        out_specs=(pl.BlockSpec((None, None, X_WIDTH, mlen), lambda l, b: (l, b, 0, 0)),
                   pl.BlockSpec((None, None, mlen, X_WIDTH), lambda l, b: (l, b, 0, 0))),
        out_shape=(jax.ShapeDtypeStruct((depth, batch, X_WIDTH, mlen), BF16),
                   jax.ShapeDtypeStruct((depth, batch, mlen, X_WIDTH), BF16)),
        compiler_params=_params(("parallel", "parallel")), name="mem_kv",
    )(mem, wkv)


def _xattn_kernel(alpha, x_ref, wq_ref, kt_ref, v_ref, wo_ref, g_ref, beta_ref, o_ref):
    x = x_ref[...]
    q = (_dot(x.astype(BF16), wq_ref[...]) * (LOG2E / math.sqrt(X_HEAD_DIM))).astype(BF16)
    heads = [slice(h * X_HEAD_DIM, (h + 1) * X_HEAD_DIM) for h in range(X_HEADS)]
    scores = [_dot(q[:, cols], kt_ref[cols, :]) for cols in heads]
    probs = []
    for s in scores:
        p = jnp.exp2(s - jnp.max(s, axis=-1, keepdims=True))
        probs.append((p.astype(BF16), jnp.sum(p, axis=-1, keepdims=True)))
    outs = [_dot(p, v_ref[:, cols]) / denom for cols, (p, denom) in zip(heads, probs)]
    att = jnp.concatenate(outs, axis=1).astype(BF16)
    h_out = _dot(att, wo_ref[...])
    o_ref[...] = _layer_norm(alpha * x + h_out, g_ref[...], beta_ref[...])


def _xattn(alpha, x, wq, kt, v, wo, lg, lb, layer, batch, seq, tm):
    n = x.shape[0]
    spb = seq // tm
    mlen = kt.shape[-1]
    lay3 = lambda b, s: (layer, 0, 0)
    in_specs = [
        pl.BlockSpec((tm, D_MODEL), lambda b, s: (b * spb + s, 0)),
        _resident((None, D_MODEL, X_WIDTH), lay3),
        pl.BlockSpec((None, None, X_WIDTH, mlen), lambda b, s: (layer, b, 0, 0)),
        pl.BlockSpec((None, None, mlen, X_WIDTH), lambda b, s: (layer, b, 0, 0)),
        _resident((None, X_WIDTH, D_MODEL), lay3),
        _resident((None, 1, D_MODEL), lay3), _resident((None, 1, D_MODEL), lay3),
    ]
    return pl.pallas_call(
        functools.partial(_xattn_kernel, alpha), grid=(batch, spb), in_specs=in_specs,
        out_specs=pl.BlockSpec((tm, D_MODEL), lambda b, s: (b * spb + s, 0)),
        out_shape=jax.ShapeDtypeStruct((n, D_MODEL), F32),
        compiler_params=_params(("parallel", "parallel")), name="cross_attn",
    )(x, wq, kt, v, wo, lg, lb)


def _ffn_kernel(alpha, xp_ref, x_ref, xn_ref, wu_ref, ck_ref, cb_ref, wd_ref, g_ref, beta_ref, o_ref,
                xe_ref, h_ref):
    tm = x_ref.shape[0]
    step = pl.program_id(1)
    last = pl.num_programs(1) - 1
    x = x_ref[...]
    nxt_row = jnp.where(step == last, 0.0, xn_ref[0:1, :])
    prv_row = jnp.where(step == 0, 0.0, xp_ref[F32_SUBLANES - 1:F32_SUBLANES, :])
    halo = jnp.concatenate([nxt_row, jnp.zeros((HALO - 2, D_MODEL), F32), prv_row], axis=0)
    xe_ref[0:tm, :] = x.astype(BF16)
    xe_ref[tm:tm + HALO, :] = halo.astype(BF16)
    nchunk = D_FF // FF_CHUNK

    def halves(c):
        return (slice(c * FF_CHUNK, (c + 1) * FF_CHUNK), slice(D_FF + c * FF_CHUNK, D_FF + (c + 1) * FF_CHUNK))

    def up(c):
        for i, cols in enumerate(halves(c)):
            h_ref[c % 2, :, i * FF_CHUNK:(i + 1) * FF_CHUNK] = _dot(xe_ref[...], wu_ref[:, cols])

    def gate(c):
        hs = h_ref.at[c % 2]
        conv = []
        for i, cols in enumerate(halves(c)):
            ck = ck_ref[:, cols]
            lanes = slice(i * FF_CHUNK, (i + 1) * FF_CHUNK)
            before = jnp.concatenate([hs[tm + HALO - 1:tm + HALO, lanes], hs[0:tm - 1, lanes]], axis=0)
            after = jnp.concatenate([hs[1:tm, lanes], hs[tm:tm + 1, lanes]], axis=0)
            conv.append(before * ck[0:1] + hs[0:tm, lanes] * ck[1:2] + after * ck[2:3] + cb_ref[:, cols])
        a = conv[0]
        return (a * (1.0 + lax.erf(a * (1.0 / math.sqrt(2.0)))) * conv[1]).astype(BF16)

    up(0)
    acc = None
    acts = []
    for c in range(nchunk):
        if c + 1 < nchunk:
            up(c + 1)
        acts.append(gate(c))
        if len(acts) == FF_DOWN_GROUP or c + 1 == nchunk:
            first = c + 1 - len(acts)
            act = acts[0] if len(acts) == 1 else jnp.concatenate(acts, axis=1)
            d = _dot(act, wd_ref[first * FF_CHUNK:(c + 1) * FF_CHUNK, :])
            acc = d if acc is None else acc + d
            acts = []
    o_ref[...] = _layer_norm(alpha * x + acc, g_ref[...], beta_ref[...])


def _ffn(alpha, x, wu, ck, cb, wd, lg, lb, layer, batch, seq, tm):
    n = x.shape[0]
    spb = seq // tm
    hpt = tm // F32_SUBLANES
    nhalo = n // F32_SUBLANES
    lay3 = lambda b, s: (layer, 0, 0)
    prev = lambda b, s: (jnp.maximum((b * spb + s) * hpt - 1, 0), 0)
    nxt = lambda b, s: (jnp.minimum((b * spb + s + 1) * hpt, nhalo - 1), 0)
    in_specs = [
        pl.BlockSpec((F32_SUBLANES, D_MODEL), prev),
        pl.BlockSpec((tm, D_MODEL), lambda b, s: (b * spb + s, 0)),
        pl.BlockSpec((F32_SUBLANES, D_MODEL), nxt),
        _resident((None, D_MODEL, 2 * D_FF), lay3),
        _resident((None, CONV_W, 2 * D_FF), lay3),
        _resident((None, 1, 2 * D_FF), lay3),
        _resident((None, D_FF, D_MODEL), lay3),
        _resident((None, 1, D_MODEL), lay3), _resident((None, 1, D_MODEL), lay3),
    ]
    return pl.pallas_call(
        functools.partial(_ffn_kernel, alpha), grid=(batch, spb), in_specs=in_specs,
        out_specs=pl.BlockSpec((tm, D_MODEL), lambda b, s: (b * spb + s, 0)),
        out_shape=jax.ShapeDtypeStruct((n, D_MODEL), F32),
        scratch_shapes=[pltpu.VMEM((tm + HALO, D_MODEL), BF16),
                        pltpu.VMEM((2, tm + HALO, 2 * FF_CHUNK), F32)],
        compiler_params=_params(("parallel", "parallel")), name="conv_ffn",
    )(x, x, x, wu, ck, cb, wd, lg, lb)


def _rope_tables(seq):
    def table(pos_parts, half):
        inv = ROPE_THETA ** (-jnp.arange(half, dtype=F32) * (1.0 / half))
        cos, sin = [], []
        for pos in pos_parts:
            ang = pos.astype(F32)[:, None] * inv[None, :]
            cos += [jnp.cos(ang), jnp.cos(ang)]
            sin += [-jnp.sin(ang), jnp.sin(ang)]
        cos = jnp.concatenate(cos, axis=1)
        sin = jnp.concatenate(sin, axis=1)
        reps = LANES // cos.shape[1]
        return jnp.tile(cos, (1, reps)), jnp.tile(sin, (1, reps))

    pos = jnp.arange(seq, dtype=jnp.int32)
    cosa, sina = table([pos], HEAD_DIM // 2)
    cosb, sinb = table([pos // GRID_W, pos % GRID_W], HEAD_DIM // 4)
    return cosa, sina, cosb, sinb


def _query_tables(tabs, b_q_gain):
    cosa, sina, cosb, sinb = (t[:, 0:HEAD_DIM] for t in tabs)
    c = LOG2E / math.sqrt(HEAD_DIM)
    quarter = HEAD_DIM // 4
    d = jnp.arange(HEAD_DIM, dtype=jnp.int32)
    partner = jnp.where((d % (2 * quarter)) < quarter, d + quarter, d - quarter)
    cat = (cosa * c).T
    sat = (sina * c).T
    cbt = jnp.swapaxes(cosb[None] * b_q_gain[:, None, :] * c, 1, 2)
    sbt = jnp.swapaxes(sinb[None] * b_q_gain[:, partner][:, None, :] * c, 1, 2)
    return cat, sat, cbt, sbt


def kernel(x, mem, w_in, b_gate, a_sink, b_q_gain, b_k_gain, c_ln_g, c_ln_b, c_ws, c_bs, w_branch, w_mix_out,
           ln1_g, ln1_b, x_wq, x_wkv, x_wo, ln2_g, ln2_b, f_w_up, f_conv_k, f_conv_b, f_w_down, ln3_g, ln3_b):
    batch, seq, _ = x.shape
    depth = w_in.shape[0]
    n = batch * seq
    alpha = (2 * depth) ** 0.25
    tm = min(ROW_TILE, seq)
    tm_ffn = min(FFN_ROW_TILE, seq)
    assert seq % tm == 0 and tm % CHUNK == 0 and seq % GRID_W == 0 and seq % tm_ffn == 0 and tm_ffn % HALO == 0
    assert D_FF % FF_CHUNK == 0
    nchunk = D_FF // FF_CHUNK

    w_main = w_in[:, :, 0:BRANCH_IN_W].astype(BF16)
    w_gate = w_in[:, :, BRANCH_IN_W:].astype(BF16)
    bg = b_gate.reshape(depth, 1, GATE_W)
    kg = jnp.tile(b_k_gain, (1, B_KV)).reshape(depth, 1, B_KV_W)
    head_of = jnp.arange(B_KV_W, dtype=jnp.int32) // HEAD_DIM
    bd = (head_of[:, None] == head_of[None, :]).astype(BF16)
    lng = c_ln_g.reshape(depth, 1, C_WIDTH)
    lnb = c_ln_b.reshape(depth, 1, C_WIDTH)
    ws = c_ws.astype(BF16)
    bsb = jnp.broadcast_to(c_bs[:, :, :, None], (depth, C_GROUPS, CHUNK, C_WIDTH // C_GROUPS))
    wb = w_branch.astype(BF16)
    wo = w_mix_out.astype(BF16)
    wq = x_wq.astype(BF16)
    wkv = x_wkv.astype(BF16)
    xwo = x_wo.astype(BF16)

    wu = f_w_up.astype(BF16)
    ck = f_conv_k
    cb = f_conv_b.reshape(depth, 1, 2 * D_FF)
    wd = (0.5 * f_w_down).astype(BF16)
    vec = lambda a: a.reshape(depth, 1, D_MODEL)
    l1g, l1b, l2g, l2b, l3g, l3b = map(vec, (ln1_g, ln1_b, ln2_g, ln2_b, ln3_g, ln3_b))
    tabs = _rope_tables(seq)
    ttabs = _query_tables(tabs, b_q_gain)
    sink_rows = jnp.repeat(a_sink * LOG2E, BLOCK, axis=1).reshape(depth, A_KV, 1, (A_HEADS // A_KV) * BLOCK)

    mkt, mv = _memkv(mem, wkv)
    xs = x.reshape(n, D_MODEL)
    for l in range(depth):
        aqt, ak, avt, bqt, bk, bvt, oc = _inproj(xs, w_main, tabs, ttabs, kg, bd, lng, lnb, ws, bsb, l, batch, seq, tm)
        oa = _attn_a(sink_rows[l], aqt, ak, avt, batch, seq)
        ob = _attn_b(bqt, bk, bvt, batch, seq)
        xs = _merge(alpha, xs, oa, ob, oc, w_gate, bg, wb, wo, l1g, l1b, l, tm)
        xs = _xattn(alpha, xs, wq, mkt, mv, xwo, l2g, l2b, l, batch, seq, tm)
        xs = _ffn(alpha, xs, wu, ck, cb, wd, l3g, l3b, l, batch, seq, tm_ffn)
    return xs.reshape(batch, seq, D_MODEL)
```

```python
import functools
import math

import jax
import jax.numpy as jnp
from jax import lax
from jax.experimental import pallas as pl
from jax.experimental.pallas import tpu as pltpu

F32 = jnp.float32
BF16 = jnp.bfloat16

D_MODEL = 1024
HEAD_DIM = 64
BLOCK = 128
A_HEADS = 8
A_KV = 2
WINDOW = 128
B_HEADS = 8
B_KV = 2
GRID_W = 64
C_WIDTH = 512
C_GROUPS = 4
CHUNK = 128
N_BRANCH = 3
ROPE_THETA = 10000.0
X_HEADS = 4
X_HEAD_DIM = 128
X_WIDTH = X_HEADS * X_HEAD_DIM
D_FF = 2816
CONV_W = 3
LN_EPS = 1e-5
RMS_EPS = 1e-6

A_Q_W = A_HEADS * HEAD_DIM
A_KV_W = A_KV * HEAD_DIM
B_Q_W = B_HEADS * HEAD_DIM
B_KV_W = B_KV * HEAD_DIM
QKV_W = A_Q_W + 2 * A_KV_W
BRANCH_IN_W = 2 * QKV_W + 2 * C_WIDTH
GATE_W = N_BRANCH * D_MODEL

LANES = 128
ROW_TILE = 1024
FFN_ROW_TILE = 512
ATTN_B_TQ = 512
ATTN_B_SUBTILES = 1
ATTN_B_SPLIT = 2
ATTN_A_QBLOCKS = 8
FF_CHUNK = 256
FF_DOWN_GROUP = 11
BF16_SUBLANES = 16
F32_SUBLANES = 8
HALO = BF16_SUBLANES
VMEM_LIMIT = 56 * 1024 * 1024
NEG_BIG = -1e30
LOG2E = 1.0 / math.log(2.0)


def _dot(a, b):
    return jnp.dot(a, b, preferred_element_type=F32)


def _layer_norm(y, g, b):
    mu = jnp.mean(y, axis=-1, keepdims=True)
    d = y - mu
    var = jnp.mean(d * d, axis=-1, keepdims=True)
    return d * lax.rsqrt(var + LN_EPS) * g + b


def _gelu(x):
    return 0.5 * x * (1.0 + lax.erf(x * (1.0 / math.sqrt(2.0))))


def _resident(shape, index_map):
    return pl.BlockSpec(shape, index_map, pipeline_mode=pl.Buffered(1))


def _params(sem):
    return pltpu.CompilerParams(dimension_semantics=sem, vmem_limit_bytes=VMEM_LIMIT)


def _inproj_kernel(x_ref, w_ref, cosa_ref, sina_ref, cosb_ref, sinb_ref, cat_ref, sat_ref, cbt_ref, sbt_ref,
                   kg_ref, bd_ref, lng_ref, lnb_ref, ws_ref, bsb_ref,
                   aqt_ref, ak_ref, avt_ref, bqt_ref, bk_ref, bvt_ref, oc_ref):
    tm = x_ref.shape[0]
    xb = x_ref[...].astype(BF16)
    lane = lax.broadcasted_iota(jnp.int32, (tm, LANES), 1)

    pa = _dot(xb, w_ref[:, 0:QKV_W])
    cosa = cosa_ref[...]
    sina = sina_ref[...]
    first_a = (lane & (HEAD_DIM - 1)) < (HEAD_DIM // 2)

    def rope_a(seg):
        sw = jnp.where(first_a, pltpu.roll(seg, LANES - HEAD_DIM // 2, 1), pltpu.roll(seg, HEAD_DIM // 2, 1))
        return seg * cosa + sw * sina

    half = HEAD_DIM // 2
    cat = cat_ref[...]
    sat = sat_ref[...]
    for j in range(A_Q_W // LANES):
        t = pa[:, j * LANES:(j + 1) * LANES].T
        for hh in range(LANES // HEAD_DIM):
            th = t[hh * HEAD_DIM:(hh + 1) * HEAD_DIM]
            sw = jnp.concatenate([th[half:], th[:half]], axis=0)
            head = j * (LANES // HEAD_DIM) + hh
            aqt_ref[head * HEAD_DIM:(head + 1) * HEAD_DIM, :] = (th * cat + sw * sat).astype(BF16)
    akr = rope_a(pa[:, A_Q_W:A_Q_W + A_KV_W]).astype(BF16)
    for g in range(A_KV):
        ak_ref[g] = akr[:, g * HEAD_DIM:(g + 1) * HEAD_DIM]
    avt_ref[...] = pa[:, A_Q_W + A_KV_W:QKV_W].T.astype(BF16)

    pb = _dot(xb, w_ref[:, QKV_W:2 * QKV_W])
    cosb = cosb_ref[...]
    sinb = sinb_ref[...]
    quarter = HEAD_DIM // 4
    first_b = (lane & (2 * quarter - 1)) < quarter

    def rope_b(seg):
        sw = jnp.where(first_b, pltpu.roll(seg, LANES - quarter, 1), pltpu.roll(seg, quarter, 1))
        return seg * cosb + sw * sinb

    def head_sumsq(v, bd):
        sq = v * v
        hi = sq.astype(BF16)
        lo = (sq - hi.astype(F32)).astype(BF16)
        return _dot(hi, bd) + _dot(lo, bd)

    cbt = cbt_ref[...]
    sbt = sbt_ref[...]
    for j in range(B_Q_W // LANES):
        t = pb[:, j * LANES:(j + 1) * LANES].T
        for hh in range(LANES // HEAD_DIM):
            th = t[hh * HEAD_DIM:(hh + 1) * HEAD_DIM]
            inv = lax.rsqrt(jnp.sum(th * th, axis=0, keepdims=True) * (1.0 / HEAD_DIM) + RMS_EPS)
            sw = jnp.concatenate([th[quarter:2 * quarter], th[:quarter], th[3 * quarter:], th[2 * quarter:3 * quarter]],
                                 axis=0)
            head = j * (LANES // HEAD_DIM) + hh
            bqt_ref[head * HEAD_DIM:(head + 1) * HEAD_DIM, :] = ((th * cbt + sw * sbt) * inv).astype(BF16)
    bk = pb[:, B_Q_W:B_Q_W + B_KV_W]
    bkn = bk * lax.rsqrt(head_sumsq(bk, bd_ref[...]) * (1.0 / HEAD_DIM) + RMS_EPS) * kg_ref[...]
    bkr = rope_b(bkn).astype(BF16)
    for g in range(B_KV):
        bk_ref[g] = bkr[:, g * HEAD_DIM:(g + 1) * HEAD_DIM]
    bvt_ref[...] = pb[:, B_Q_W + B_KV_W:QKV_W].T.astype(BF16)

    pc = _gelu(_dot(xb, w_ref[:, 2 * QKV_W:BRANCH_IN_W]))
    u = pc[:, 0:C_WIDTH]
    vln = _layer_norm(pc[:, C_WIDTH:2 * C_WIDTH], lng_ref[...], lnb_ref[...]).astype(BF16)
    nch = tm // CHUNK
    gw = C_WIDTH // C_GROUPS
    for g in range(C_GROUPS):
        cs = slice(g * gw, (g + 1) * gw)
        rhs = jnp.concatenate([vln[r * CHUNK:(r + 1) * CHUNK, cs] for r in range(nch)], axis=1)
        mixed = _dot(ws_ref[g], rhs)
        for r in range(nch):
            rs = slice(r * CHUNK, (r + 1) * CHUNK)
            oc_ref[rs, cs] = (u[rs, cs] * (mixed[:, r * gw:(r + 1) * gw] + bsb_ref[g])).astype(BF16)


def _inproj(x, w, tabs, ttabs, kg, bd, lng, lnb, ws, bsb, layer, batch, seq, tm):
    n = x.shape[0]
    spb = seq // tm
    row = lambda i: (i, 0)
    tab = lambda i: (i % spb, 0)
    lay3 = lambda i: (layer, 0, 0)
    lay4 = lambda i: (layer, 0, 0, 0)
    tr = lambda i: (i // spb, 0, i % spb)
    out_shape = (
        jax.ShapeDtypeStruct((batch, A_Q_W, seq), BF16),
        jax.ShapeDtypeStruct((A_KV, n, HEAD_DIM), BF16),
        jax.ShapeDtypeStruct((batch, A_KV_W, seq), BF16),
        jax.ShapeDtypeStruct((batch, B_Q_W, seq), BF16),
        jax.ShapeDtypeStruct((B_KV, n, HEAD_DIM), BF16),
        jax.ShapeDtypeStruct((batch, B_KV_W, seq), BF16),
        jax.ShapeDtypeStruct((n, C_WIDTH), BF16),
    )
    in_specs = [
        pl.BlockSpec((tm, D_MODEL), row),
        _resident((None, D_MODEL, BRANCH_IN_W), lay3),
        pl.BlockSpec((tm, LANES), tab), pl.BlockSpec((tm, LANES), tab),
        pl.BlockSpec((tm, LANES), tab), pl.BlockSpec((tm, LANES), tab),
        pl.BlockSpec((HEAD_DIM, tm), lambda i: (0, i % spb)), pl.BlockSpec((HEAD_DIM, tm), lambda i: (0, i % spb)),
        pl.BlockSpec((None, HEAD_DIM, tm), lambda i: (layer, 0, i % spb)),
        pl.BlockSpec((None, HEAD_DIM, tm), lambda i: (layer, 0, i % spb)),
        _resident((None, 1, B_KV_W), lay3),
        _resident((B_KV_W, B_KV_W), lambda i: (0, 0)),
        _resident((None, 1, C_WIDTH), lay3), _resident((None, 1, C_WIDTH), lay3),
        _resident((None, C_GROUPS, CHUNK, CHUNK), lay4),
        _resident((None, C_GROUPS, CHUNK, C_WIDTH // C_GROUPS), lay4),
    ]
    out_specs = (
        pl.BlockSpec((None, A_Q_W, tm), tr),
        pl.BlockSpec((A_KV, tm, HEAD_DIM), lambda i: (0, i, 0)),
        pl.BlockSpec((None, A_KV_W, tm), tr),
        pl.BlockSpec((None, B_Q_W, tm), tr),
        pl.BlockSpec((B_KV, tm, HEAD_DIM), lambda i: (0, i, 0)),
        pl.BlockSpec((None, B_KV_W, tm), tr),
        pl.BlockSpec((tm, C_WIDTH), row),
    )
    return pl.pallas_call(
        _inproj_kernel, grid=(n // tm,), in_specs=in_specs, out_specs=out_specs, out_shape=out_shape,
        compiler_params=_params(("parallel",)), name="inproj",
    )(x, w, *tabs, *ttabs, kg, bd, lng, lnb, ws, bsb)


def _attn_a_kernel(sink_ref, qt_ref, kl_ref, kc_ref, kr_ref, vtl_ref, vtc_ref, vtr_ref, ones_ref, o_ref):
    qblocks = qt_ref.shape[1] // BLOCK
    group = A_HEADS // A_KV
    step = pl.program_id(1)
    last = pl.num_programs(1) - 1
    span = 3 * BLOCK
    kj = lax.broadcasted_iota(jnp.int32, (span, group * BLOCK), 0)
    qi = lax.broadcasted_iota(jnp.int32, (span, group * BLOCK), 1) & (BLOCK - 1)
    band = (kj >= qi) & (kj <= qi + 2 * WINDOW)
    units = [(g, t) for g in range(A_KV) for t in range(qblocks)]

    scores = []
    for g, t in units:
        kspan = jnp.concatenate([kl_ref[g], kc_ref[g], kr_ref[g]], axis=0)
        qt = jnp.concatenate(
            [qt_ref[(g * group + h) * HEAD_DIM:(g * group + h + 1) * HEAD_DIM, t * BLOCK:(t + 1) * BLOCK]
             for h in range(group)], axis=1)
        scores.append(_dot(kspan[t * BLOCK:t * BLOCK + span], qt))
    probs = []
    for (g, t), st in zip(units, scores):
        valid = band
        if t == 0:
            valid = valid & (kj >= jnp.where(step == 0, BLOCK, 0))
        if t == qblocks - 1:
            valid = valid & (kj < jnp.where(step == last, 2 * BLOCK, span))
        st = jnp.where(valid, st, NEG_BIG)
        m = jnp.maximum(jnp.max(st, axis=0, keepdims=True), sink_ref[g])
        probs.append((jnp.exp2(st - m).astype(BF16), jnp.exp2(sink_ref[g] - m)))
    for (g, t), (pt, sink_p) in zip(units, probs):
        rows = slice(g * HEAD_DIM, (g + 1) * HEAD_DIM)
        vts = jnp.concatenate([vtl_ref[rows, :], vtc_ref[rows, :], vtr_ref[rows, :]], axis=1)
        vte = jnp.concatenate([vts, ones_ref[...]], axis=0)
        oe = _dot(vte[:, t * BLOCK:t * BLOCK + span], pt)
        o = oe[0:HEAD_DIM] / (oe[HEAD_DIM:HEAD_DIM + 1] + sink_p)
        stacked = jnp.concatenate([o[:, h * BLOCK:(h + 1) * BLOCK] for h in range(group)], axis=0)
        o_ref[t * BLOCK:(t + 1) * BLOCK, g * group * HEAD_DIM:(g + 1) * group * HEAD_DIM] = (
            stacked.T.astype(BF16))


def _attn_a(sink_rows, qt, k, vt, batch, seq):
    n = batch * seq
    qb = ATTN_A_QBLOCKS if seq % (ATTN_A_QBLOCKS * BLOCK) == 0 else 1
    tq = qb * BLOCK
    steps = seq // tq
    nblk = seq // BLOCK
    left = lambda s: jnp.maximum(s * qb - 1, 0)
    right = lambda s: jnp.minimum((s + 1) * qb, nblk - 1)
    ones = (lax.broadcasted_iota(jnp.int32, (BF16_SUBLANES, tq + 2 * BLOCK), 0) == 0).astype(BF16)
    in_specs = [
        _resident(sink_rows.shape, lambda b, s: (0, 0, 0)),
        pl.BlockSpec((None, A_Q_W, tq), lambda b, s: (b, 0, s)),
        pl.BlockSpec((A_KV, BLOCK, HEAD_DIM), lambda b, s: (0, b * nblk + left(s), 0)),
        pl.BlockSpec((A_KV, tq, HEAD_DIM), lambda b, s: (0, b * steps + s, 0)),
        pl.BlockSpec((A_KV, BLOCK, HEAD_DIM), lambda b, s: (0, b * nblk + right(s), 0)),
        pl.BlockSpec((None, A_KV_W, BLOCK), lambda b, s: (b, 0, left(s))),
        pl.BlockSpec((None, A_KV_W, tq), lambda b, s: (b, 0, s)),
        pl.BlockSpec((None, A_KV_W, BLOCK), lambda b, s: (b, 0, right(s))),
        _resident(ones.shape, lambda b, s: (0, 0)),
    ]
    return pl.pallas_call(
        _attn_a_kernel, grid=(batch, steps), in_specs=in_specs,
        out_specs=pl.BlockSpec((tq, A_Q_W), lambda b, s: (b * steps + s, 0)),
        out_shape=jax.ShapeDtypeStruct((n, A_Q_W), BF16),
        compiler_params=_params(("parallel", "parallel")), name="attn_window",
    )(sink_rows, qt, k, k, k, vt, vt, vt, ones)


def _attn_b_kernel(qt_ref, k_ref, vt_ref, ones_ref, o_ref):
    group = B_HEADS // B_KV
    ones = ones_ref[...]
    tq = min(ATTN_B_TQ, qt_ref.shape[1])
    units = [(sub, head) for sub in range(qt_ref.shape[1] // tq) for head in range(B_HEADS)]

    hw = tq // ATTN_B_SPLIT

    def scores(u):
        sub, head = units[u]
        return [_dot(k_ref[head // group],
                     qt_ref[head * HEAD_DIM:(head + 1) * HEAD_DIM, sub * tq + i * hw:sub * tq + (i + 1) * hw])
                for i in range(ATTN_B_SPLIT)]

    outs = []
    st_next = scores(0)
    for u, (sub, head) in enumerate(units):
        st = st_next
        if u + 1 < len(units):
            st_next = scores(u + 1)
        g = head // group
        vte = jnp.concatenate([vt_ref[g * HEAD_DIM:(g + 1) * HEAD_DIM, :], ones], axis=0)
        parts = []
        for sth in st:
            m = jnp.max(sth, axis=0, keepdims=True)
            pt = jnp.exp2(sth - m).astype(BF16)
            oe = _dot(vte, pt)
            parts.append(oe[0:HEAD_DIM] / oe[HEAD_DIM:HEAD_DIM + 1])
        outs.append(jnp.concatenate(parts, axis=1))
        if head == B_HEADS - 1:
            o_ref[sub * tq:(sub + 1) * tq, :] = jnp.concatenate(outs, axis=0).T.astype(BF16)
            outs = []


def _attn_b(qt, k, vt, batch, seq):
    n = batch * seq
    tq = min(ATTN_B_TQ * ATTN_B_SUBTILES, seq)
    steps = seq // tq
    ones = (lax.broadcasted_iota(jnp.int32, (BF16_SUBLANES, seq), 0) == 0).astype(BF16)
    in_specs = [
        pl.BlockSpec((None, B_Q_W, tq), lambda b, s: (b, 0, s)),
        pl.BlockSpec((B_KV, seq, HEAD_DIM), lambda b, s: (0, b, 0)),
        pl.BlockSpec((None, B_KV_W, seq), lambda b, s: (b, 0, 0)),
        _resident((BF16_SUBLANES, seq), lambda b, s: (0, 0)),
    ]
    return pl.pallas_call(
        _attn_b_kernel, grid=(batch, steps), in_specs=in_specs,
        out_specs=pl.BlockSpec((tq, B_Q_W), lambda b, s: (b * steps + s, 0)),
        out_shape=jax.ShapeDtypeStruct((n, B_Q_W), BF16),
        compiler_params=_params(("parallel", "parallel")), name="attn_dense",
    )(qt, k, vt, ones)


def _merge_kernel(alpha, x_ref, a_ref, b_ref, c_ref, wg_ref, bg_ref, wb_ref, wo_ref, g_ref, beta_ref, o_ref):
    x = x_ref[...]
    xb = x.astype(BF16)
    merged = None
    for i, br in enumerate((a_ref, b_ref, c_ref)):
        cols = slice(i * D_MODEL, (i + 1) * D_MODEL)
        gate = jax.nn.sigmoid(_dot(xb, wg_ref[:, cols]) + bg_ref[:, cols])
        t = _dot(br[...], wb_ref[i]) * gate
        merged = t if merged is None else merged + t
    h = _dot(merged.astype(BF16), wo_ref[...])
    o_ref[...] = _layer_norm(alpha * x + h, g_ref[...], beta_ref[...])


def _merge(alpha, x, oa, ob, oc, wg, bg, wb, wo, lg, lb, layer, tm):
    n = x.shape[0]
    row = lambda i: (i, 0)
    lay3 = lambda i: (layer, 0, 0)
    lay4 = lambda i: (layer, 0, 0, 0)
    in_specs = [
        pl.BlockSpec((tm, D_MODEL), row),
        pl.BlockSpec((tm, A_Q_W), row), pl.BlockSpec((tm, B_Q_W), row), pl.BlockSpec((tm, C_WIDTH), row),
        _resident((None, D_MODEL, GATE_W), lay3),
        _resident((None, 1, GATE_W), lay3),
        _resident((None, N_BRANCH, C_WIDTH, D_MODEL), lay4),
        _resident((None, D_MODEL, D_MODEL), lay3),
        _resident((None, 1, D_MODEL), lay3), _resident((None, 1, D_MODEL), lay3),
    ]
    return pl.pallas_call(
        functools.partial(_merge_kernel, alpha), grid=(n // tm,), in_specs=in_specs,
        out_specs=pl.BlockSpec((tm, D_MODEL), row),
        out_shape=jax.ShapeDtypeStruct((n, D_MODEL), F32),
        compiler_params=_params(("parallel",)), name="merge",
    )(x, oa, ob, oc, wg, bg, wb, wo, lg, lb)


def _memkv_kernel(m_ref, w_ref, kt_ref, v_ref):
    kv = _dot(m_ref[...].astype(BF16), w_ref[...])
    kt_ref[...] = kv[:, 0:X_WIDTH].T.astype(BF16)
    v_ref[...] = kv[:, X_WIDTH:2 * X_WIDTH].astype(BF16)


def _memkv(mem, wkv):
    batch, mlen, _ = mem.shape
    depth = wkv.shape[0]
    return pl.pallas_call(
        _memkv_kernel, grid=(depth, batch),
        in_specs=[pl.BlockSpec((None, mlen, D_MODEL), lambda l, b: (b, 0, 0)),
                  pl.BlockSpec((None, D_MODEL, 2 * X_WIDTH), lambda l, b: (l, 0, 0))],
        out_specs=(pl.BlockSpec((None, None, X_WIDTH, mlen), lambda l, b: (l, b, 0, 0)),
                   pl.BlockSpec((None, None, mlen, X_WIDTH), lambda l, b: (l, b, 0, 0))),
        out_shape=(jax.ShapeDtypeStruct((depth, batch, X_WIDTH, mlen), BF16),
                   jax.ShapeDtypeStruct((depth, batch, mlen, X_WIDTH), BF16)),
        compiler_params=_params(("parallel", "parallel")), name="mem_kv",
    )(mem, wkv)


def _xattn_kernel(alpha, x_ref, wq_ref, kt_ref, v_ref, wo_ref, g_ref, beta_ref, o_ref):
    x = x_ref[...]
    q = (_dot(x.astype(BF16), wq_ref[...]) * (LOG2E / math.sqrt(X_HEAD_DIM))).astype(BF16)
    heads = [slice(h * X_HEAD_DIM, (h + 1) * X_HEAD_DIM) for h in range(X_HEADS)]
    scores = [_dot(q[:, cols], kt_ref[cols, :]) for cols in heads]
    probs = []
    for s in scores:
        p = jnp.exp2(s - jnp.max(s, axis=-1, keepdims=True))
        probs.append((p.astype(BF16), jnp.sum(p, axis=-1, keepdims=True)))
    outs = [_dot(p, v_ref[:, cols]) / denom for cols, (p, denom) in zip(heads, probs)]
    att = jnp.concatenate(outs, axis=1).astype(BF16)
    h_out = _dot(att, wo_ref[...])
    o_ref[...] = _layer_norm(alpha * x + h_out, g_ref[...], beta_ref[...])


def _xattn(alpha, x, wq, kt, v, wo, lg, lb, layer, batch, seq, tm):
    n = x.shape[0]
    spb = seq // tm
    mlen = kt.shape[-1]
    lay3 = lambda b, s: (layer, 0, 0)
    in_specs = [
        pl.BlockSpec((tm, D_MODEL), lambda b, s: (b * spb + s, 0)),
        _resident((None, D_MODEL, X_WIDTH), lay3),
        pl.BlockSpec((None, None, X_WIDTH, mlen), lambda b, s: (layer, b, 0, 0)),
        pl.BlockSpec((None, None, mlen, X_WIDTH), lambda b, s: (layer, b, 0, 0)),
        _resident((None, X_WIDTH, D_MODEL), lay3),
        _resident((None, 1, D_MODEL), lay3), _resident((None, 1, D_MODEL), lay3),
    ]
    return pl.pallas_call(
        functools.partial(_xattn_kernel, alpha), grid=(batch, spb), in_specs=in_specs,
        out_specs=pl.BlockSpec((tm, D_MODEL), lambda b, s: (b * spb + s, 0)),
        out_shape=jax.ShapeDtypeStruct((n, D_MODEL), F32),
        compiler_params=_params(("parallel", "parallel")), name="cross_attn",
    )(x, wq, kt, v, wo, lg, lb)


def _ffn_kernel(alpha, xp_ref, x_ref, xn_ref, wu_ref, ck_ref, cb_ref, wd_ref, g_ref, beta_ref, o_ref,
                xe_ref, h_ref):
    tm = x_ref.shape[0]
    step = pl.program_id(1)
    last = pl.num_programs(1) - 1
    x = x_ref[...]
    nxt_row = jnp.where(step == last, 0.0, xn_ref[0:1, :])
    prv_row = jnp.where(step == 0, 0.0, xp_ref[F32_SUBLANES - 1:F32_SUBLANES, :])
    halo = jnp.concatenate([nxt_row, jnp.zeros((HALO - 2, D_MODEL), F32), prv_row], axis=0)
    xe_ref[0:tm, :] = x.astype(BF16)
    xe_ref[tm:tm + HALO, :] = halo.astype(BF16)
    nchunk = D_FF // FF_CHUNK

    def halves(c):
        return (slice(c * FF_CHUNK, (c + 1) * FF_CHUNK), slice(D_FF + c * FF_CHUNK, D_FF + (c + 1) * FF_CHUNK))

    def up(c):
        for i, cols in enumerate(halves(c)):
            h_ref[c % 2, :, i * FF_CHUNK:(i + 1) * FF_CHUNK] = _dot(xe_ref[...], wu_ref[:, cols])

    def gate(c):
        hs = h_ref.at[c % 2]
        conv = []
        for i, cols in enumerate(halves(c)):
            ck = ck_ref[:, cols]
            lanes = slice(i * FF_CHUNK, (i + 1) * FF_CHUNK)
            before = jnp.concatenate([hs[tm + HALO - 1:tm + HALO, lanes], hs[0:tm - 1, lanes]], axis=0)
            after = jnp.concatenate([hs[1:tm, lanes], hs[tm:tm + 1, lanes]], axis=0)
            conv.append(before * ck[0:1] + hs[0:tm, lanes] * ck[1:2] + after * ck[2:3] + cb_ref[:, cols])
        a = conv[0]
        return (a * (1.0 + lax.erf(a * (1.0 / math.sqrt(2.0)))) * conv[1]).astype(BF16)

    up(0)
    acc = None
    acts = []
    for c in range(nchunk):
        if c + 1 < nchunk:
            up(c + 1)
        acts.append(gate(c))
        if len(acts) == FF_DOWN_GROUP or c + 1 == nchunk:
            first = c + 1 - len(acts)
            act = acts[0] if len(acts) == 1 else jnp.concatenate(acts, axis=1)
            d = _dot(act, wd_ref[first * FF_CHUNK:(c + 1) * FF_CHUNK, :])
            acc = d if acc is None else acc + d
            acts = []
    o_ref[...] = _layer_norm(alpha * x + acc, g_ref[...], beta_ref[...])


def _ffn(alpha, x, wu, ck, cb, wd, lg, lb, layer, batch, seq, tm):
    n = x.shape[0]
    spb = seq // tm
    hpt = tm // F32_SUBLANES
    nhalo = n // F32_SUBLANES
    lay3 = lambda b, s: (layer, 0, 0)
    prev = lambda b, s: (jnp.maximum((b * spb + s) * hpt - 1, 0), 0)
    nxt = lambda b, s: (jnp.minimum((b * spb + s + 1) * hpt, nhalo - 1), 0)
    in_specs = [
        pl.BlockSpec((F32_SUBLANES, D_MODEL), prev),
        pl.BlockSpec((tm, D_MODEL), lambda b, s: (b * spb + s, 0)),
        pl.BlockSpec((F32_SUBLANES, D_MODEL), nxt),
        _resident((None, D_MODEL, 2 * D_FF), lay3),
        _resident((None, CONV_W, 2 * D_FF), lay3),
        _resident((None, 1, 2 * D_FF), lay3),
        _resident((None, D_FF, D_MODEL), lay3),
        _resident((None, 1, D_MODEL), lay3), _resident((None, 1, D_MODEL), lay3),
    ]
    return pl.pallas_call(
        functools.partial(_ffn_kernel, alpha), grid=(batch, spb), in_specs=in_specs,
        out_specs=pl.BlockSpec((tm, D_MODEL), lambda b, s: (b * spb + s, 0)),
        out_shape=jax.ShapeDtypeStruct((n, D_MODEL), F32),
        scratch_shapes=[pltpu.VMEM((tm + HALO, D_MODEL), BF16),
                        pltpu.VMEM((2, tm + HALO, 2 * FF_CHUNK), F32)],
        compiler_params=_params(("parallel", "parallel")), name="conv_ffn",
    )(x, x, x, wu, ck, cb, wd, lg, lb)


def _rope_tables(seq):
    def table(pos_parts, half):
        inv = ROPE_THETA ** (-jnp.arange(half, dtype=F32) * (1.0 / half))
        cos, sin = [], []
        for pos in pos_parts:
            ang = pos.astype(F32)[:, None] * inv[None, :]
            cos += [jnp.cos(ang), jnp.cos(ang)]
            sin += [-jnp.sin(ang), jnp.sin(ang)]
        cos = jnp.concatenate(cos, axis=1)
        sin = jnp.concatenate(sin, axis=1)
        reps = LANES // cos.shape[1]
        return jnp.tile(cos, (1, reps)), jnp.tile(sin, (1, reps))

    pos = jnp.arange(seq, dtype=jnp.int32)
    cosa, sina = table([pos], HEAD_DIM // 2)
    cosb, sinb = table([pos // GRID_W, pos % GRID_W], HEAD_DIM // 4)
    return cosa, sina, cosb, sinb


def _query_tables(tabs, b_q_gain):
    cosa, sina, cosb, sinb = (t[:, 0:HEAD_DIM] for t in tabs)
    c = LOG2E / math.sqrt(HEAD_DIM)
    quarter = HEAD_DIM // 4
    d = jnp.arange(HEAD_DIM, dtype=jnp.int32)
    partner = jnp.where((d % (2 * quarter)) < quarter, d + quarter, d - quarter)
    cat = (cosa * c).T
    sat = (sina * c).T
    cbt = jnp.swapaxes(cosb[None] * b_q_gain[:, None, :] * c, 1, 2)
    sbt = jnp.swapaxes(sinb[None] * b_q_gain[:, partner][:, None, :] * c, 1, 2)
    return cat, sat, cbt, sbt


def kernel(x, mem, w_in, b_gate, a_sink, b_q_gain, b_k_gain, c_ln_g, c_ln_b, c_ws, c_bs, w_branch, w_mix_out,
           ln1_g, ln1_b, x_wq, x_wkv, x_wo, ln2_g, ln2_b, f_w_up, f_conv_k, f_conv_b, f_w_down, ln3_g, ln3_b):
    batch, seq, _ = x.shape
    depth = w_in.shape[0]
    n = batch * seq
    alpha = (2 * depth) ** 0.25
    tm = min(ROW_TILE, seq)
    tm_ffn = min(FFN_ROW_TILE, seq)
    assert seq % tm == 0 and tm % CHUNK == 0 and seq % GRID_W == 0 and seq % tm_ffn == 0 and tm_ffn % HALO == 0
    assert D_FF % FF_CHUNK == 0
    nchunk = D_FF // FF_CHUNK

    w_main = w_in[:, :, 0:BRANCH_IN_W].astype(BF16)
    w_gate = w_in[:, :, BRANCH_IN_W:].astype(BF16)
    bg = b_gate.reshape(depth, 1, GATE_W)
    kg = jnp.tile(b_k_gain, (1, B_KV)).reshape(depth, 1, B_KV_W)
    head_of = jnp.arange(B_KV_W, dtype=jnp.int32) // HEAD_DIM
    bd = (head_of[:, None] == head_of[None, :]).astype(BF16)
    lng = c_ln_g.reshape(depth, 1, C_WIDTH)
    lnb = c_ln_b.reshape(depth, 1, C_WIDTH)
    ws = c_ws.astype(BF16)
    bsb = jnp.broadcast_to(c_bs[:, :, :, None], (depth, C_GROUPS, CHUNK, C_WIDTH // C_GROUPS))
    wb = w_branch.astype(BF16)
    wo = w_mix_out.astype(BF16)
    wq = x_wq.astype(BF16)
    wkv = x_wkv.astype(BF16)
    xwo = x_wo.astype(BF16)

    wu = f_w_up.astype(BF16)
    ck = f_conv_k
    cb = f_conv_b.reshape(depth, 1, 2 * D_FF)
    wd = (0.5 * f_w_down).astype(BF16)
    vec = lambda a: a.reshape(depth, 1, D_MODEL)
    l1g, l1b, l2g, l2b, l3g, l3b = map(vec, (ln1_g, ln1_b, ln2_g, ln2_b, ln3_g, ln3_b))
    tabs = _rope_tables(seq)
    ttabs = _query_tables(tabs, b_q_gain)
    sink_rows = jnp.repeat(a_sink * LOG2E, BLOCK, axis=1).reshape(depth, A_KV, 1, (A_HEADS // A_KV) * BLOCK)

    mkt, mv = _memkv(mem, wkv)
    xs = x.reshape(n, D_MODEL)
    for l in range(depth):
        aqt, ak, avt, bqt, bk, bvt, oc = _inproj(xs, w_main, tabs, ttabs, kg, bd, lng, lnb, ws, bsb, l, batch, seq, tm)
        oa = _attn_a(sink_rows[l], aqt, ak, avt, batch, seq)
        ob = _attn_b(bqt, bk, bvt, batch, seq)
        xs = _merge(alpha, xs, oa, ob, oc, w_gate, bg, wb, wo, l1g, l1b, l, tm)
        xs = _xattn(alpha, xs, wq, mkt, mv, xwo, l2g, l2b, l, batch, seq, tm)
        xs = _ffn(alpha, xs, wu, ck, cb, wd, l3g, l3b, l, batch, seq, tm_ffn)
    return xs.reshape(batch, seq, D_MODEL)
```

```python
import functools
import math

import jax
import jax.numpy as jnp
from jax import lax
from jax.experimental import pallas as pl
from jax.experimental.pallas import tpu as pltpu

F32 = jnp.float32
BF16 = jnp.bfloat16

D_MODEL = 1024
HEAD_DIM = 64
BLOCK = 128
A_HEADS = 8
A_KV = 2
WINDOW = 128
B_HEADS = 8
B_KV = 2
GRID_W = 64
C_WIDTH = 512
C_GROUPS = 4
CHUNK = 128
N_BRANCH = 3
ROPE_THETA = 10000.0
X_HEADS = 4
X_HEAD_DIM = 128
X_WIDTH = X_HEADS * X_HEAD_DIM
D_FF = 2816
CONV_W = 3
LN_EPS = 1e-5
RMS_EPS = 1e-6

A_Q_W = A_HEADS * HEAD_DIM
A_KV_W = A_KV * HEAD_DIM
B_Q_W = B_HEADS * HEAD_DIM
B_KV_W = B_KV * HEAD_DIM
QKV_W = A_Q_W + 2 * A_KV_W
BRANCH_IN_W = 2 * QKV_W + 2 * C_WIDTH
GATE_W = N_BRANCH * D_MODEL

LANES = 128
ROW_TILE = 1024
FFN_ROW_TILE = 512
ATTN_B_TQ = 512
ATTN_B_SUBTILES = 1
ATTN_A_QBLOCKS = 8
FF_CHUNK = 256
FF_DOWN_GROUP = 11
BF16_SUBLANES = 16
F32_SUBLANES = 8
HALO = BF16_SUBLANES
VMEM_LIMIT = 56 * 1024 * 1024
NEG_BIG = -1e30
LOG2E = 1.0 / math.log(2.0)


def _dot(a, b):
    return jnp.dot(a, b, preferred_element_type=F32)


def _layer_norm(y, g, b):
    mu = jnp.mean(y, axis=-1, keepdims=True)
    d = y - mu
    var = jnp.mean(d * d, axis=-1, keepdims=True)
    return d * lax.rsqrt(var + LN_EPS) * g + b


def _gelu(x):
    return 0.5 * x * (1.0 + lax.erf(x * (1.0 / math.sqrt(2.0))))


def _resident(shape, index_map):
    return pl.BlockSpec(shape, index_map, pipeline_mode=pl.Buffered(1))


def _params(sem):
    return pltpu.CompilerParams(dimension_semantics=sem, vmem_limit_bytes=VMEM_LIMIT)


def _inproj_kernel(x_ref, w_ref, cosa_ref, sina_ref, cosb_ref, sinb_ref, cat_ref, sat_ref, cbt_ref, sbt_ref,
                   kg_ref, bd_ref, lng_ref, lnb_ref, ws_ref, bsb_ref,
                   aqt_ref, ak_ref, avt_ref, bqt_ref, bk_ref, bvt_ref, oc_ref):
    tm = x_ref.shape[0]
    xb = x_ref[...].astype(BF16)
    lane = lax.broadcasted_iota(jnp.int32, (tm, LANES), 1)

    pa = _dot(xb, w_ref[:, 0:QKV_W])
    cosa = cosa_ref[...]
    sina = sina_ref[...]
    first_a = (lane & (HEAD_DIM - 1)) < (HEAD_DIM // 2)

    def rope_a(seg):
        sw = jnp.where(first_a, pltpu.roll(seg, LANES - HEAD_DIM // 2, 1), pltpu.roll(seg, HEAD_DIM // 2, 1))
        return seg * cosa + sw * sina

    half = HEAD_DIM // 2
    cat = cat_ref[...]
    sat = sat_ref[...]
    for j in range(A_Q_W // LANES):
        t = pa[:, j * LANES:(j + 1) * LANES].T
        for hh in range(LANES // HEAD_DIM):
            th = t[hh * HEAD_DIM:(hh + 1) * HEAD_DIM]
            sw = jnp.concatenate([th[half:], th[:half]], axis=0)
            head = j * (LANES // HEAD_DIM) + hh
            aqt_ref[head * HEAD_DIM:(head + 1) * HEAD_DIM, :] = (th * cat + sw * sat).astype(BF16)
    akr = rope_a(pa[:, A_Q_W:A_Q_W + A_KV_W]).astype(BF16)
    for g in range(A_KV):
        ak_ref[g] = akr[:, g * HEAD_DIM:(g + 1) * HEAD_DIM]
    avt_ref[...] = pa[:, A_Q_W + A_KV_W:QKV_W].T.astype(BF16)

    pb = _dot(xb, w_ref[:, QKV_W:2 * QKV_W])
    cosb = cosb_ref[...]
    sinb = sinb_ref[...]
    quarter = HEAD_DIM // 4
    first_b = (lane & (2 * quarter - 1)) < quarter

    def rope_b(seg):
        sw = jnp.where(first_b, pltpu.roll(seg, LANES - quarter, 1), pltpu.roll(seg, quarter, 1))
        return seg * cosb + sw * sinb

    def head_sumsq(v, bd):
        sq = v * v
        hi = sq.astype(BF16)
        lo = (sq - hi.astype(F32)).astype(BF16)
        return _dot(hi, bd) + _dot(lo, bd)

    cbt = cbt_ref[...]
    sbt = sbt_ref[...]
    for j in range(B_Q_W // LANES):
        t = pb[:, j * LANES:(j + 1) * LANES].T
        for hh in range(LANES // HEAD_DIM):
            th = t[hh * HEAD_DIM:(hh + 1) * HEAD_DIM]
            inv = lax.rsqrt(jnp.sum(th * th, axis=0, keepdims=True) * (1.0 / HEAD_DIM) + RMS_EPS)
            sw = jnp.concatenate([th[quarter:2 * quarter], th[:quarter], th[3 * quarter:], th[2 * quarter:3 * quarter]],
                                 axis=0)
            head = j * (LANES // HEAD_DIM) + hh
            bqt_ref[head * HEAD_DIM:(head + 1) * HEAD_DIM, :] = ((th * cbt + sw * sbt) * inv).astype(BF16)
    bk = pb[:, B_Q_W:B_Q_W + B_KV_W]
    bkn = bk * lax.rsqrt(head_sumsq(bk, bd_ref[...]) * (1.0 / HEAD_DIM) + RMS_EPS) * kg_ref[...]
    bkr = rope_b(bkn).astype(BF16)
    for g in range(B_KV):
        bk_ref[g] = bkr[:, g * HEAD_DIM:(g + 1) * HEAD_DIM]
    bvt_ref[...] = pb[:, B_Q_W + B_KV_W:QKV_W].T.astype(BF16)

    pc = _gelu(_dot(xb, w_ref[:, 2 * QKV_W:BRANCH_IN_W]))
    u = pc[:, 0:C_WIDTH]
    vln = _layer_norm(pc[:, C_WIDTH:2 * C_WIDTH], lng_ref[...], lnb_ref[...]).astype(BF16)
    nch = tm // CHUNK
    gw = C_WIDTH // C_GROUPS
    for g in range(C_GROUPS):
        cs = slice(g * gw, (g + 1) * gw)
        rhs = jnp.concatenate([vln[r * CHUNK:(r + 1) * CHUNK, cs] for r in range(nch)], axis=1)
        mixed = _dot(ws_ref[g], rhs)
        for r in range(nch):
            rs = slice(r * CHUNK, (r + 1) * CHUNK)
            oc_ref[rs, cs] = (u[rs, cs] * (mixed[:, r * gw:(r + 1) * gw] + bsb_ref[g])).astype(BF16)


def _inproj(x, w, tabs, ttabs, kg, bd, lng, lnb, ws, bsb, layer, batch, seq, tm):
    n = x.shape[0]
    spb = seq // tm
    row = lambda i: (i, 0)
    tab = lambda i: (i % spb, 0)
    lay3 = lambda i: (layer, 0, 0)
    lay4 = lambda i: (layer, 0, 0, 0)
    tr = lambda i: (i // spb, 0, i % spb)
    out_shape = (
        jax.ShapeDtypeStruct((batch, A_Q_W, seq), BF16),
        jax.ShapeDtypeStruct((A_KV, n, HEAD_DIM), BF16),
        jax.ShapeDtypeStruct((batch, A_KV_W, seq), BF16),
        jax.ShapeDtypeStruct((batch, B_Q_W, seq), BF16),
        jax.ShapeDtypeStruct((B_KV, n, HEAD_DIM), BF16),
        jax.ShapeDtypeStruct((batch, B_KV_W, seq), BF16),
        jax.ShapeDtypeStruct((n, C_WIDTH), BF16),
    )
    in_specs = [
        pl.BlockSpec((tm, D_MODEL), row),
        _resident((None, D_MODEL, BRANCH_IN_W), lay3),
        pl.BlockSpec((tm, LANES), tab), pl.BlockSpec((tm, LANES), tab),
        pl.BlockSpec((tm, LANES), tab), pl.BlockSpec((tm, LANES), tab),
        pl.BlockSpec((HEAD_DIM, tm), lambda i: (0, i % spb)), pl.BlockSpec((HEAD_DIM, tm), lambda i: (0, i % spb)),
        pl.BlockSpec((None, HEAD_DIM, tm), lambda i: (layer, 0, i % spb)),
        pl.BlockSpec((None, HEAD_DIM, tm), lambda i: (layer, 0, i % spb)),
        _resident((None, 1, B_KV_W), lay3),
        _resident((B_KV_W, B_KV_W), lambda i: (0, 0)),
        _resident((None, 1, C_WIDTH), lay3), _resident((None, 1, C_WIDTH), lay3),
        _resident((None, C_GROUPS, CHUNK, CHUNK), lay4),
        _resident((None, C_GROUPS, CHUNK, C_WIDTH // C_GROUPS), lay4),
    ]
    out_specs = (
        pl.BlockSpec((None, A_Q_W, tm), tr),
        pl.BlockSpec((A_KV, tm, HEAD_DIM), lambda i: (0, i, 0)),
        pl.BlockSpec((None, A_KV_W, tm), tr),
        pl.BlockSpec((None, B_Q_W, tm), tr),
        pl.BlockSpec((B_KV, tm, HEAD_DIM), lambda i: (0, i, 0)),
        pl.BlockSpec((None, B_KV_W, tm), tr),
        pl.BlockSpec((tm, C_WIDTH), row),
    )
    return pl.pallas_call(
        _inproj_kernel, grid=(n // tm,), in_specs=in_specs, out_specs=out_specs, out_shape=out_shape,
        compiler_params=_params(("parallel",)), name="inproj",
    )(x, w, *tabs, *ttabs, kg, bd, lng, lnb, ws, bsb)


def _attn_a_kernel(sink_ref, qt_ref, kl_ref, kc_ref, kr_ref, vtl_ref, vtc_ref, vtr_ref, ones_ref, o_ref):
    qblocks = qt_ref.shape[1] // BLOCK
    group = A_HEADS // A_KV
    step = pl.program_id(1)
    last = pl.num_programs(1) - 1
    span = 3 * BLOCK
    kj = lax.broadcasted_iota(jnp.int32, (span, group * BLOCK), 0)
    qi = lax.broadcasted_iota(jnp.int32, (span, group * BLOCK), 1) & (BLOCK - 1)
    band = (kj >= qi) & (kj <= qi + 2 * WINDOW)
    units = [(g, t) for g in range(A_KV) for t in range(qblocks)]

    scores = []
    for g, t in units:
        kspan = jnp.concatenate([kl_ref[g], kc_ref[g], kr_ref[g]], axis=0)
        qt = jnp.concatenate(
            [qt_ref[(g * group + h) * HEAD_DIM:(g * group + h + 1) * HEAD_DIM, t * BLOCK:(t + 1) * BLOCK]
             for h in range(group)], axis=1)
        scores.append(_dot(kspan[t * BLOCK:t * BLOCK + span], qt))
    probs = []
    for (g, t), st in zip(units, scores):
        valid = band
        if t == 0:
            valid = valid & (kj >= jnp.where(step == 0, BLOCK, 0))
        if t == qblocks - 1:
            valid = valid & (kj < jnp.where(step == last, 2 * BLOCK, span))
        st = jnp.where(valid, st, NEG_BIG)
        m = jnp.maximum(jnp.max(st, axis=0, keepdims=True), sink_ref[g])
        probs.append((jnp.exp2(st - m).astype(BF16), jnp.exp2(sink_ref[g] - m)))
    for (g, t), (pt, sink_p) in zip(units, probs):
        rows = slice(g * HEAD_DIM, (g + 1) * HEAD_DIM)
        vts = jnp.concatenate([vtl_ref[rows, :], vtc_ref[rows, :], vtr_ref[rows, :]], axis=1)
        vte = jnp.concatenate([vts, ones_ref[...]], axis=0)
        oe = _dot(vte[:, t * BLOCK:t * BLOCK + span], pt)
        o = oe[0:HEAD_DIM] / (oe[HEAD_DIM:HEAD_DIM + 1] + sink_p)
        stacked = jnp.concatenate([o[:, h * BLOCK:(h + 1) * BLOCK] for h in range(group)], axis=0)
        o_ref[t * BLOCK:(t + 1) * BLOCK, g * group * HEAD_DIM:(g + 1) * group * HEAD_DIM] = (
            stacked.T.astype(BF16))


def _attn_a(sink_rows, qt, k, vt, batch, seq):
    n = batch * seq
    qb = ATTN_A_QBLOCKS if seq % (ATTN_A_QBLOCKS * BLOCK) == 0 else 1
    tq = qb * BLOCK
    steps = seq // tq
    nblk = seq // BLOCK
    left = lambda s: jnp.maximum(s * qb - 1, 0)
    right = lambda s: jnp.minimum((s + 1) * qb, nblk - 1)
    ones = (lax.broadcasted_iota(jnp.int32, (BF16_SUBLANES, tq + 2 * BLOCK), 0) == 0).astype(BF16)
    in_specs = [
        _resident(sink_rows.shape, lambda b, s: (0, 0, 0)),
        pl.BlockSpec((None, A_Q_W, tq), lambda b, s: (b, 0, s)),
        pl.BlockSpec((A_KV, BLOCK, HEAD_DIM), lambda b, s: (0, b * nblk + left(s), 0)),
        pl.BlockSpec((A_KV, tq, HEAD_DIM), lambda b, s: (0, b * steps + s, 0)),
        pl.BlockSpec((A_KV, BLOCK, HEAD_DIM), lambda b, s: (0, b * nblk + right(s), 0)),
        pl.BlockSpec((None, A_KV_W, BLOCK), lambda b, s: (b, 0, left(s))),
        pl.BlockSpec((None, A_KV_W, tq), lambda b, s: (b, 0, s)),
        pl.BlockSpec((None, A_KV_W, BLOCK), lambda b, s: (b, 0, right(s))),
        _resident(ones.shape, lambda b, s: (0, 0)),
    ]
    return pl.pallas_call(
        _attn_a_kernel, grid=(batch, steps), in_specs=in_specs,
        out_specs=pl.BlockSpec((tq, A_Q_W), lambda b, s: (b * steps + s, 0)),
        out_shape=jax.ShapeDtypeStruct((n, A_Q_W), BF16),
        compiler_params=_params(("parallel", "parallel")), name="attn_window",
    )(sink_rows, qt, k, k, k, vt, vt, vt, ones)


def _attn_b_kernel(qt_ref, k_ref, vt_ref, ones_ref, o_ref):
    group = B_HEADS // B_KV
    ones = ones_ref[...]
    tq = min(ATTN_B_TQ, qt_ref.shape[1])
    units = [(sub, head) for sub in range(qt_ref.shape[1] // tq) for head in range(B_HEADS)]

    def scores(u):
        sub, head = units[u]
        return _dot(k_ref[head // group],
                    qt_ref[head * HEAD_DIM:(head + 1) * HEAD_DIM, sub * tq:(sub + 1) * tq])

    outs = []
    st_next = scores(0)
    for u, (sub, head) in enumerate(units):
        st = st_next
        if u + 1 < len(units):
            st_next = scores(u + 1)
        g = head // group
        vte = jnp.concatenate([vt_ref[g * HEAD_DIM:(g + 1) * HEAD_DIM, :], ones], axis=0)
        m = jnp.max(st, axis=0, keepdims=True)
        pt = jnp.exp2(st - m).astype(BF16)
        oe = _dot(vte, pt)
        outs.append(oe[0:HEAD_DIM] / oe[HEAD_DIM:HEAD_DIM + 1])
        if head == B_HEADS - 1:
            o_ref[sub * tq:(sub + 1) * tq, :] = jnp.concatenate(outs, axis=0).T.astype(BF16)
            outs = []


def _attn_b(qt, k, vt, batch, seq):
    n = batch * seq
    tq = min(ATTN_B_TQ * ATTN_B_SUBTILES, seq)
    steps = seq // tq
    ones = (lax.broadcasted_iota(jnp.int32, (BF16_SUBLANES, seq), 0) == 0).astype(BF16)
    in_specs = [
        pl.BlockSpec((None, B_Q_W, tq), lambda b, s: (b, 0, s)),
        pl.BlockSpec((B_KV, seq, HEAD_DIM), lambda b, s: (0, b, 0)),
        pl.BlockSpec((None, B_KV_W, seq), lambda b, s: (b, 0, 0)),
        _resident((BF16_SUBLANES, seq), lambda b, s: (0, 0)),
    ]
    return pl.pallas_call(
        _attn_b_kernel, grid=(batch, steps), in_specs=in_specs,
        out_specs=pl.BlockSpec((tq, B_Q_W), lambda b, s: (b * steps + s, 0)),
        out_shape=jax.ShapeDtypeStruct((n, B_Q_W), BF16),
        compiler_params=_params(("parallel", "parallel")), name="attn_dense",
    )(qt, k, vt, ones)


def _merge_kernel(alpha, x_ref, a_ref, b_ref, c_ref, wg_ref, bg_ref, wb_ref, wo_ref, g_ref, beta_ref, o_ref):
    x = x_ref[...]
    xb = x.astype(BF16)
    merged = None
    for i, br in enumerate((a_ref, b_ref, c_ref)):
        cols = slice(i * D_MODEL, (i + 1) * D_MODEL)
        gate = jax.nn.sigmoid(_dot(xb, wg_ref[:, cols]) + bg_ref[:, cols])
        t = _dot(br[...], wb_ref[i]) * gate
        merged = t if merged is None else merged + t
    h = _dot(merged.astype(BF16), wo_ref[...])
    o_ref[...] = _layer_norm(alpha * x + h, g_ref[...], beta_ref[...])


def _merge_xattn_kernel(alpha, x_ref, a_ref, b_ref, c_ref, wg_ref, bg_ref, wb_ref, wo_ref, g1_ref, b1_ref,
                        wq_ref, kt_ref, v_ref, xwo_ref, g2_ref, b2_ref, o_ref):
    x = x_ref[...]
    xb = x.astype(BF16)
    merged = None
    for i, br in enumerate((a_ref, b_ref, c_ref)):
        cols = slice(i * D_MODEL, (i + 1) * D_MODEL)
        gate = jax.nn.sigmoid(_dot(xb, wg_ref[:, cols]) + bg_ref[:, cols])
        t = _dot(br[...], wb_ref[i]) * gate
        merged = t if merged is None else merged + t
    x1 = _layer_norm(alpha * x + _dot(merged.astype(BF16), wo_ref[...]), g1_ref[...], b1_ref[...])
    q = (_dot(x1.astype(BF16), wq_ref[...]) * (LOG2E / math.sqrt(X_HEAD_DIM))).astype(BF16)
    heads = [slice(h * X_HEAD_DIM, (h + 1) * X_HEAD_DIM) for h in range(X_HEADS)]
    scores = [_dot(q[:, cols], kt_ref[cols, :]) for cols in heads]
    probs = []
    for s in scores:
        p = jnp.exp2(s - jnp.max(s, axis=-1, keepdims=True))
        probs.append((p.astype(BF16), jnp.sum(p, axis=-1, keepdims=True)))
    outs = [_dot(p, v_ref[:, cols]) / denom for cols, (p, denom) in zip(heads, probs)]
    att = jnp.concatenate(outs, axis=1).astype(BF16)
    o_ref[...] = _layer_norm(alpha * x1 + _dot(att, xwo_ref[...]), g2_ref[...], b2_ref[...])


def _merge_xattn(alpha, x, oa, ob, oc, wg, bg, wb, wo, l1g, l1b, wq, kt, v, xwo, l2g, l2b, layer, batch, seq, tm):
    n = x.shape[0]
    spb = seq // tm
    mlen = kt.shape[-1]
    row = lambda b, s: (b * spb + s, 0)
    lay3 = lambda b, s: (layer, 0, 0)
    lay4 = lambda b, s: (layer, 0, 0, 0)
    in_specs = [
        pl.BlockSpec((tm, D_MODEL), row),
        pl.BlockSpec((tm, A_Q_W), row), pl.BlockSpec((tm, B_Q_W), row), pl.BlockSpec((tm, C_WIDTH), row),
        _resident((None, D_MODEL, GATE_W), lay3),
        _resident((None, 1, GATE_W), lay3),
        _resident((None, N_BRANCH, C_WIDTH, D_MODEL), lay4),
        _resident((None, D_MODEL, D_MODEL), lay3),
        _resident((None, 1, D_MODEL), lay3), _resident((None, 1, D_MODEL), lay3),
        _resident((None, D_MODEL, X_WIDTH), lay3),
        pl.BlockSpec((None, None, X_WIDTH, mlen), lambda b, s: (layer, b, 0, 0)),
        pl.BlockSpec((None, None, mlen, X_WIDTH), lambda b, s: (layer, b, 0, 0)),
        _resident((None, X_WIDTH, D_MODEL), lay3),
        _resident((None, 1, D_MODEL), lay3), _resident((None, 1, D_MODEL), lay3),
    ]
    return pl.pallas_call(
        functools.partial(_merge_xattn_kernel, alpha), grid=(batch, spb), in_specs=in_specs,
        out_specs=pl.BlockSpec((tm, D_MODEL), row),
        out_shape=jax.ShapeDtypeStruct((n, D_MODEL), F32),
        compiler_params=_params(("parallel", "parallel")), name="merge_cross",
    )(x, oa, ob, oc, wg, bg, wb, wo, l1g, l1b, wq, kt, v, xwo, l2g, l2b)


def _merge(alpha, x, oa, ob, oc, wg, bg, wb, wo, lg, lb, layer, tm):
    n = x.shape[0]
    row = lambda i: (i, 0)
    lay3 = lambda i: (layer, 0, 0)
    lay4 = lambda i: (layer, 0, 0, 0)
    in_specs = [
        pl.BlockSpec((tm, D_MODEL), row),
        pl.BlockSpec((tm, A_Q_W), row), pl.BlockSpec((tm, B_Q_W), row), pl.BlockSpec((tm, C_WIDTH), row),
        _resident((None, D_MODEL, GATE_W), lay3),
        _resident((None, 1, GATE_W), lay3),
        _resident((None, N_BRANCH, C_WIDTH, D_MODEL), lay4),
        _resident((None, D_MODEL, D_MODEL), lay3),
        _resident((None, 1, D_MODEL), lay3), _resident((None, 1, D_MODEL), lay3),
    ]
    return pl.pallas_call(
        functools.partial(_merge_kernel, alpha), grid=(n // tm,), in_specs=in_specs,
        out_specs=pl.BlockSpec((tm, D_MODEL), row),
        out_shape=jax.ShapeDtypeStruct((n, D_MODEL), F32),
        compiler_params=_params(("parallel",)), name="merge",
    )(x, oa, ob, oc, wg, bg, wb, wo, lg, lb)


def _memkv_kernel(m_ref, w_ref, kt_ref, v_ref):
    kv = _dot(m_ref[...].astype(BF16), w_ref[...])
    kt_ref[...] = kv[:, 0:X_WIDTH].T.astype(BF16)
    v_ref[...] = kv[:, X_WIDTH:2 * X_WIDTH].astype(BF16)


def _memkv(mem, wkv):
    batch, mlen, _ = mem.shape
    depth = wkv.shape[0]
    return pl.pallas_call(
        _memkv_kernel, grid=(depth, batch),
        in_specs=[pl.BlockSpec((None, mlen, D_MODEL), lambda l, b: (b, 0, 0)),
                  pl.BlockSpec((None, D_MODEL, 2 * X_WIDTH), lambda l, b: (l, 0, 0))],
        out_specs=(pl.BlockSpec((None, None, X_WIDTH, mlen), lambda l, b: (l, b, 0, 0)),
                   pl.BlockSpec((None, None, mlen, X_WIDTH), lambda l, b: (l, b, 0, 0))),
        out_shape=(jax.ShapeDtypeStruct((depth, batch, X_WIDTH, mlen), BF16),
                   jax.ShapeDtypeStruct((depth, batch, mlen, X_WIDTH), BF16)),
        compiler_params=_params(("parallel", "parallel")), name="mem_kv",
    )(mem, wkv)


def _xattn_kernel(alpha, x_ref, wq_ref, kt_ref, v_ref, wo_ref, g_ref, beta_ref, o_ref):
    x = x_ref[...]
    q = (_dot(x.astype(BF16), wq_ref[...]) * (LOG2E / math.sqrt(X_HEAD_DIM))).astype(BF16)
    heads = [slice(h * X_HEAD_DIM, (h + 1) * X_HEAD_DIM) for h in range(X_HEADS)]
    scores = [_dot(q[:, cols], kt_ref[cols, :]) for cols in heads]
    probs = []
    for s in scores:
        p = jnp.exp2(s - jnp.max(s, axis=-1, keepdims=True))
        probs.append((p.astype(BF16), jnp.sum(p, axis=-1, keepdims=True)))
    outs = [_dot(p, v_ref[:, cols]) / denom for cols, (p, denom) in zip(heads, probs)]
    att = jnp.concatenate(outs, axis=1).astype(BF16)
    h_out = _dot(att, wo_ref[...])
    o_ref[...] = _layer_norm(alpha * x + h_out, g_ref[...], beta_ref[...])


def _xattn(alpha, x, wq, kt, v, wo, lg, lb, layer, batch, seq, tm):
    n = x.shape[0]
    spb = seq // tm
    mlen = kt.shape[-1]
    lay3 = lambda b, s: (layer, 0, 0)
    in_specs = [
        pl.BlockSpec((tm, D_MODEL), lambda b, s: (b * spb + s, 0)),
        _resident((None, D_MODEL, X_WIDTH), lay3),
        pl.BlockSpec((None, None, X_WIDTH, mlen), lambda b, s: (layer, b, 0, 0)),
        pl.BlockSpec((None, None, mlen, X_WIDTH), lambda b, s: (layer, b, 0, 0)),
        _resident((None, X_WIDTH, D_MODEL), lay3),
        _resident((None, 1, D_MODEL), lay3), _resident((None, 1, D_MODEL), lay3),
    ]
    return pl.pallas_call(
        functools.partial(_xattn_kernel, alpha), grid=(batch, spb), in_specs=in_specs,
        out_specs=pl.BlockSpec((tm, D_MODEL), lambda b, s: (b * spb + s, 0)),
        out_shape=jax.ShapeDtypeStruct((n, D_MODEL), F32),
        compiler_params=_params(("parallel", "parallel")), name="cross_attn",
    )(x, wq, kt, v, wo, lg, lb)


def _ffn_kernel(alpha, xp_ref, x_ref, xn_ref, wu_ref, ck_ref, cb_ref, wd_ref, g_ref, beta_ref, o_ref,
                xe_ref, h_ref):
    tm = x_ref.shape[0]
    step = pl.program_id(1)
    last = pl.num_programs(1) - 1
    x = x_ref[...]
    nxt_row = jnp.where(step == last, 0.0, xn_ref[0:1, :])
    prv_row = jnp.where(step == 0, 0.0, xp_ref[F32_SUBLANES - 1:F32_SUBLANES, :])
    halo = jnp.concatenate([nxt_row, jnp.zeros((HALO - 2, D_MODEL), F32), prv_row], axis=0)
    xe_ref[0:tm, :] = x.astype(BF16)
    xe_ref[tm:tm + HALO, :] = halo.astype(BF16)
    nchunk = D_FF // FF_CHUNK

    def halves(c):
        return (slice(c * FF_CHUNK, (c + 1) * FF_CHUNK), slice(D_FF + c * FF_CHUNK, D_FF + (c + 1) * FF_CHUNK))

    def up(c):
        for i, cols in enumerate(halves(c)):
            h_ref[c % 2, :, i * FF_CHUNK:(i + 1) * FF_CHUNK] = _dot(xe_ref[...], wu_ref[:, cols])

    def gate(c):
        hs = h_ref.at[c % 2]
        conv = []
        for i, cols in enumerate(halves(c)):
            ck = ck_ref[:, cols]
            lanes = slice(i * FF_CHUNK, (i + 1) * FF_CHUNK)
            before = jnp.concatenate([hs[tm + HALO - 1:tm + HALO, lanes], hs[0:tm - 1, lanes]], axis=0)
            after = jnp.concatenate([hs[1:tm, lanes], hs[tm:tm + 1, lanes]], axis=0)
            conv.append(before * ck[0:1] + hs[0:tm, lanes] * ck[1:2] + after * ck[2:3] + cb_ref[:, cols])
        a = conv[0]
        return (a * (1.0 + lax.erf(a * (1.0 / math.sqrt(2.0)))) * conv[1]).astype(BF16)

    up(0)
    acc = None
    acts = []
    for c in range(nchunk):
        if c + 1 < nchunk:
            up(c + 1)
        acts.append(gate(c))
        if len(acts) == FF_DOWN_GROUP or c + 1 == nchunk:
            first = c + 1 - len(acts)
            act = acts[0] if len(acts) == 1 else jnp.concatenate(acts, axis=1)
            d = _dot(act, wd_ref[first * FF_CHUNK:(c + 1) * FF_CHUNK, :])
            acc = d if acc is None else acc + d
            acts = []
    o_ref[...] = _layer_norm(alpha * x + acc, g_ref[...], beta_ref[...])


def _ffn(alpha, x, wu, ck, cb, wd, lg, lb, layer, batch, seq, tm):
    n = x.shape[0]
    spb = seq // tm
    hpt = tm // F32_SUBLANES
    nhalo = n // F32_SUBLANES
    lay3 = lambda b, s: (layer, 0, 0)
    prev = lambda b, s: (jnp.maximum((b * spb + s) * hpt - 1, 0), 0)
    nxt = lambda b, s: (jnp.minimum((b * spb + s + 1) * hpt, nhalo - 1), 0)
    in_specs = [
        pl.BlockSpec((F32_SUBLANES, D_MODEL), prev),
        pl.BlockSpec((tm, D_MODEL), lambda b, s: (b * spb + s, 0)),
        pl.BlockSpec((F32_SUBLANES, D_MODEL), nxt),
        _resident((None, D_MODEL, 2 * D_FF), lay3),
        _resident((None, CONV_W, 2 * D_FF), lay3),
        _resident((None, 1, 2 * D_FF), lay3),
        _resident((None, D_FF, D_MODEL), lay3),
        _resident((None, 1, D_MODEL), lay3), _resident((None, 1, D_MODEL), lay3),
    ]
    return pl.pallas_call(
        functools.partial(_ffn_kernel, alpha), grid=(batch, spb), in_specs=in_specs,
        out_specs=pl.BlockSpec((tm, D_MODEL), lambda b, s: (b * spb + s, 0)),
        out_shape=jax.ShapeDtypeStruct((n, D_MODEL), F32),
        scratch_shapes=[pltpu.VMEM((tm + HALO, D_MODEL), BF16),
                        pltpu.VMEM((2, tm + HALO, 2 * FF_CHUNK), F32)],
        compiler_params=_params(("parallel", "parallel")), name="conv_ffn",
    )(x, x, x, wu, ck, cb, wd, lg, lb)


def _rope_tables(seq):
    def table(pos_parts, half):
        inv = ROPE_THETA ** (-jnp.arange(half, dtype=F32) * (1.0 / half))
        cos, sin = [], []
        for pos in pos_parts:
            ang = pos.astype(F32)[:, None] * inv[None, :]
            cos += [jnp.cos(ang), jnp.cos(ang)]
            sin += [-jnp.sin(ang), jnp.sin(ang)]
        cos = jnp.concatenate(cos, axis=1)
        sin = jnp.concatenate(sin, axis=1)
        reps = LANES // cos.shape[1]
        return jnp.tile(cos, (1, reps)), jnp.tile(sin, (1, reps))

    pos = jnp.arange(seq, dtype=jnp.int32)
    cosa, sina = table([pos], HEAD_DIM // 2)
    cosb, sinb = table([pos // GRID_W, pos % GRID_W], HEAD_DIM // 4)
    return cosa, sina, cosb, sinb


def _query_tables(tabs, b_q_gain):
    cosa, sina, cosb, sinb = (t[:, 0:HEAD_DIM] for t in tabs)
    c = LOG2E / math.sqrt(HEAD_DIM)
    quarter = HEAD_DIM // 4
    d = jnp.arange(HEAD_DIM, dtype=jnp.int32)
    partner = jnp.where((d % (2 * quarter)) < quarter, d + quarter, d - quarter)
    cat = (cosa * c).T
    sat = (sina * c).T
    cbt = jnp.swapaxes(cosb[None] * b_q_gain[:, None, :] * c, 1, 2)
    sbt = jnp.swapaxes(sinb[None] * b_q_gain[:, partner][:, None, :] * c, 1, 2)
    return cat, sat, cbt, sbt


def kernel(x, mem, w_in, b_gate, a_sink, b_q_gain, b_k_gain, c_ln_g, c_ln_b, c_ws, c_bs, w_branch, w_mix_out,
           ln1_g, ln1_b, x_wq, x_wkv, x_wo, ln2_g, ln2_b, f_w_up, f_conv_k, f_conv_b, f_w_down, ln3_g, ln3_b):
    batch, seq, _ = x.shape
    depth = w_in.shape[0]
    n = batch * seq
    alpha = (2 * depth) ** 0.25
    tm = min(ROW_TILE, seq)
    tm_ffn = min(FFN_ROW_TILE, seq)
    assert seq % tm == 0 and tm % CHUNK == 0 and seq % GRID_W == 0 and seq % tm_ffn == 0 and tm_ffn % HALO == 0
    assert D_FF % FF_CHUNK == 0
    nchunk = D_FF // FF_CHUNK

    w_main = w_in[:, :, 0:BRANCH_IN_W].astype(BF16)
    w_gate = w_in[:, :, BRANCH_IN_W:].astype(BF16)
    bg = b_gate.reshape(depth, 1, GATE_W)
    kg = jnp.tile(b_k_gain, (1, B_KV)).reshape(depth, 1, B_KV_W)
    head_of = jnp.arange(B_KV_W, dtype=jnp.int32) // HEAD_DIM
    bd = (head_of[:, None] == head_of[None, :]).astype(BF16)
    lng = c_ln_g.reshape(depth, 1, C_WIDTH)
    lnb = c_ln_b.reshape(depth, 1, C_WIDTH)
    ws = c_ws.astype(BF16)
    bsb = jnp.broadcast_to(c_bs[:, :, :, None], (depth, C_GROUPS, CHUNK, C_WIDTH // C_GROUPS))
    wb = w_branch.astype(BF16)
    wo = w_mix_out.astype(BF16)
    wq = x_wq.astype(BF16)
    wkv = x_wkv.astype(BF16)
    xwo = x_wo.astype(BF16)

    wu = f_w_up.astype(BF16)
    ck = f_conv_k
    cb = f_conv_b.reshape(depth, 1, 2 * D_FF)
    wd = (0.5 * f_w_down).astype(BF16)
    vec = lambda a: a.reshape(depth, 1, D_MODEL)
    l1g, l1b, l2g, l2b, l3g, l3b = map(vec, (ln1_g, ln1_b, ln2_g, ln2_b, ln3_g, ln3_b))
    tabs = _rope_tables(seq)
    ttabs = _query_tables(tabs, b_q_gain)
    sink_rows = jnp.repeat(a_sink * LOG2E, BLOCK, axis=1).reshape(depth, A_KV, 1, (A_HEADS // A_KV) * BLOCK)

    mkt, mv = _memkv(mem, wkv)
    xs = x.reshape(n, D_MODEL)
    for l in range(depth):
        aqt, ak, avt, bqt, bk, bvt, oc = _inproj(xs, w_main, tabs, ttabs, kg, bd, lng, lnb, ws, bsb, l, batch, seq, tm)
        oa = _attn_a(sink_rows[l], aqt, ak, avt, batch, seq)
        ob = _attn_b(bqt, bk, bvt, batch, seq)
        xs = _merge_xattn(alpha, xs, oa, ob, oc, w_gate, bg, wb, wo, l1g, l1b,
                          wq, mkt, mv, xwo, l2g, l2b, l, batch, seq, tm)
        xs = _ffn(alpha, xs, wu, ck, cb, wd, l3g, l3b, l, batch, seq, tm_ffn)
    return xs.reshape(batch, seq, D_MODEL)
```
